```python
import jax, jax.numpy as jnp
from jax import lax
import numpy as np

D_MODEL = 1024
BATCH = 16
SEQ = 256
DEPTH = 4
DEC_BATCH = 8
DEC_SEQ = 2048
PAST_LEN = 512

GRID_W = 64
EPS = 1e-6
F_MIN = 1e-20
CHUNK = 64
H_A = 8
DK_A = 128
DV_A = D_MODEL // H_A
F_A = H_A * DK_A
DVT_A = H_A * DV_A
HGRN_IN = F_A + DVT_A + F_A + F_A + DVT_A
H_B = 4
DK_B = D_MODEL // 2 // H_B
DV_B = D_MODEL // H_B
KT_B = H_B * DK_B
DVT_B = H_B * DV_B
QKV_B = 2 * KT_B + DVT_B
GLA_RANK = 16
GATE_NORM = 16.0
CONV_K = 3
GLA_IN = QKV_B + DVT_B + 2 * GLA_RANK
P_HEADS = 8
N_KEYS = 128
N_EXPERTS = N_KEYS * N_KEYS
P_TOPK = 16
P_QDIM = 256
P_HALF = P_QDIM // 2
PEER_CHUNK = 128

N_A = (DEPTH + 1) // 2
N_B = DEPTH // 2

kernel_name = "hybrid_hgrn2_gla_peer_diffusion_step"

f32 = jnp.float32


def _rms_norm(x, w):
    xf = x.astype(f32)
    y = xf * lax.rsqrt(jnp.mean(xf * xf, axis=-1, keepdims=True) + EPS)
    return (y * w.astype(f32)).astype(x.dtype)


def _adaln(cvec, w, b):
    mod = jax.nn.silu(cvec) @ w + b
    return jnp.split(mod, 6, axis=-1)


def _modulate(h, shift, scale):
    return h * (1.0 + scale[:, None, :]) + shift[:, None, :]


def _chunk_scan(q, k, v, g, s0):
    b_, L, H, DK = q.shape
    DV = v.shape[-1]
    n = L // CHUNK

    def to_chunks(a):
        return a.astype(f32).reshape(b_, n, CHUNK, H, a.shape[-1]).transpose(1, 0, 3, 2, 4)

    qc, kc, vc, gc = to_chunks(q), to_chunks(k), to_chunks(v), to_chunks(g)
    causal = jnp.tril(jnp.ones((CHUNK, CHUNK), bool))[:, :, None]

    def step(S, inp):
        qi, ki, vi, gi = inp
        bcum = jnp.cumsum(gi, axis=2)
        b_last = bcum[:, :, -1:, :]
        o_inter = jnp.einsum('bhtd,bhde->bhte', qi * jnp.exp(bcum), S)
        diff = bcum[:, :, :, None, :] - bcum[:, :, None, :, :]
        decay = jnp.where(causal, jnp.exp(jnp.minimum(diff, 0.0)), 0.0)
        att = jnp.einsum('bhtd,bhtsd,bhsd->bhts', qi, decay, ki)
        o = o_inter + jnp.einsum('bhts,bhse->bhte', att, vi)
        S = jnp.exp(b_last[:, :, 0, :])[..., None] * S + jnp.einsum(
            'bhsd,bhse->bhde', ki * jnp.exp(b_last - bcum), vi)
        return S, o

    S, o = lax.scan(step, s0.astype(f32), (qc, kc, vc, gc))
    o = o.transpose(1, 0, 3, 2, 4).reshape(b_, L, H, DV)
    return o, S


def _bidirectional_scan(q, v, fwd, bwd):
    k_f, g_f, s_f = fwd
    k_b, g_b, s_b = bwd
    o_f, S_f = _chunk_scan(q, k_f, v, g_f, s_f)
    rev = lambda a: jnp.flip(a, axis=1)
    o_b, S_b = _chunk_scan(rev(q), rev(k_b), rev(v), rev(g_b), s_b)
    return o_f + rev(o_b), S_f, S_b


def _gated_head_norm(o, gate, w):
    of = o.astype(f32)
    y = of * lax.rsqrt(jnp.mean(of * of, axis=-1, keepdims=True) + EPS)
    return y * w.astype(f32) * jax.nn.silu(gate.astype(f32))


def _hgrn2_mixer(h, w_in, lb, gnorm_w, w_out, s_f, s_b):
    b_, L, _ = h.shape
    z = h @ w_in
    q, iv, f_f, f_b, og = jnp.split(
        z, [F_A, F_A + DVT_A, 2 * F_A + DVT_A, 3 * F_A + DVT_A], axis=-1)
    q = jax.nn.silu(q).reshape(b_, L, H_A, DK_A)
    iv = iv.reshape(b_, L, H_A, DV_A)

    def gate(f_pre, lb_d):
        f_pre = f_pre.astype(f32)
        f = lb_d + (1.0 - lb_d) * jax.nn.sigmoid(f_pre)
        log_f = jnp.log(jnp.maximum(f, F_MIN))
        k = (1.0 - lb_d) * jax.nn.sigmoid(-f_pre)
        return k.reshape(b_, L, H_A, DK_A), log_f.reshape(b_, L, H_A, DK_A)

    k_f, g_f = gate(f_f, lb[0])
    k_b, g_b = gate(f_b, lb[1])
    o, S_f, S_b = _bidirectional_scan(q, iv, (k_f, g_f, s_f), (k_b, g_b, s_b))
    o = _gated_head_norm(o, og.reshape(b_, L, H_A, DV_A), gnorm_w)
    return o.reshape(b_, L, DVT_A).astype(h.dtype) @ w_out, S_f, S_b


def _depthwise_conv2d(x, w, rows):
    b_, L, C = x.shape
    xg = x.reshape(b_, rows, L // rows, C)
    y = lax.conv_general_dilated(
        xg, w[:, :, None, :].astype(x.dtype), (1, 1), 'SAME',
        dimension_numbers=('NHWC', 'HWIO', 'NHWC'), feature_group_count=C)
    return y.reshape(b_, L, C)


def _gla_mixer(h, rows, w_in, conv_w, w_gate, b_gate, gnorm_w, w_out, s_f, s_b):
    b_, L, _ = h.shape
    z = h @ w_in
    qkv, og, low = jnp.split(z, [QKV_B, QKV_B + DVT_B], axis=-1)
    qkv = jax.nn.silu(_depthwise_conv2d(qkv, conv_w, rows))
    q, k, v = jnp.split(qkv, [KT_B, 2 * KT_B], axis=-1)
    q = (q * (DK_B ** -0.5)).reshape(b_, L, H_B, DK_B)
    k = k.reshape(b_, L, H_B, DK_B)
    v = v.reshape(b_, L, H_B, DV_B)
    low = low.reshape(b_, L, 2, GLA_RANK)
    gk = jnp.einsum('blnr,nrk->blnk', low, w_gate) + b_gate
    log_g = jax.nn.log_sigmoid(gk.astype(f32)) / GATE_NORM
    g_f = log_g[:, :, 0].reshape(b_, L, H_B, DK_B)
    g_b = log_g[:, :, 1].reshape(b_, L, H_B, DK_B)
    o, S_f, S_b = _bidirectional_scan(q, v, (k, g_f, s_f), (k, g_b, s_b))
    o = _gated_head_norm(o, og.reshape(b_, L, H_B, DV_B), gnorm_w)
    return o.reshape(b_, L, DVT_B).astype(h.dtype) @ w_out, S_f, S_b


def _peer(h, w_q, sub_keys, u_tab, v_tab):
    b_, L, D = h.shape
    blocks = h.reshape(-1, PEER_CHUNK, D)

    def one(hc):
        q = (hc @ w_q).reshape(PEER_CHUNK, P_HEADS, 2, P_HALF)
        s = jnp.einsum('thpk,hpnk->thpn', q, sub_keys).astype(f32)
        s_top, i_top = lax.top_k(s, P_TOPK)
        comb = (s_top[:, :, 0, :, None] + s_top[:, :, 1, None, :]).reshape(
            PEER_CHUNK, P_HEADS, P_TOPK * P_TOPK)
        c_top, c_pos = lax.top_k(comb, P_TOPK)
        idx = (jnp.take_along_axis(i_top[:, :, 0], c_pos // P_TOPK, axis=-1) * N_KEYS
               + jnp.take_along_axis(i_top[:, :, 1], c_pos % P_TOPK, axis=-1))
        gates = jax.nn.softmax(c_top, axis=-1).reshape(PEER_CHUNK, P_HEADS * P_TOPK)
        idx = idx.reshape(PEER_CHUNK, P_HEADS * P_TOPK)
        u = jnp.take(u_tab, idx, axis=0)
        act = jax.nn.gelu(jnp.einsum('td,ted->te', hc, u), approximate=False)
        coef = gates.astype(hc.dtype) * act
        vv = jnp.take(v_tab, idx, axis=0)
        return jnp.einsum('te,ted->td', coef, vv)

    return lax.map(one, blocks).reshape(b_, L, D)


def setup_inputs(seed: int = 0) -> dict:
    key = jax.random.key(seed)
    ks = jax.random.split(key, 26)
    nrm = lambda k, shape, s: jax.random.normal(k, shape, jnp.float32) * s
    return {
        "x_prompt": nrm(ks[0], (BATCH, SEQ, D_MODEL), 1.0),
        "x_sample": nrm(ks[1], (DEC_BATCH, DEC_SEQ, D_MODEL), 1.0),
        "state_hgrn": nrm(ks[2], (DEC_BATCH, N_A, 2, H_A, DK_A, DV_A), 0.5),
        "state_gla": nrm(ks[3], (DEC_BATCH, N_B, 2, H_B, DK_B, DV_B), 1.0),
        "c": nrm(ks[4], (DEC_BATCH, D_MODEL), 1.0),
        "c_ctx": nrm(ks[5], (D_MODEL,), 1.0),
        "ada_w": nrm(ks[6], (DEPTH, D_MODEL, 6 * D_MODEL), 0.5 * D_MODEL ** -0.5),
        "ada_b": nrm(ks[7], (DEPTH, 6 * D_MODEL), 0.02),
        "norm1_w": 1.0 + nrm(ks[8], (DEPTH, D_MODEL), 0.02),
        "norm2_w": 1.0 + nrm(ks[9], (DEPTH, D_MODEL), 0.02),
        "final_norm_w": 1.0 + nrm(ks[10], (D_MODEL,), 0.02),
        "hgrn_w_in": nrm(ks[11], (N_A, D_MODEL, HGRN_IN), D_MODEL ** -0.5),
        "hgrn_lb": nrm(ks[12], (N_A, 2, F_A), 0.5),
        "hgrn_gnorm_w": 1.0 + nrm(ks[13], (N_A, DV_A), 0.02),
        "hgrn_w_out": nrm(ks[14], (N_A, DVT_A, D_MODEL), DVT_A ** -0.5),
        "gla_w_in": nrm(ks[15], (N_B, D_MODEL, GLA_IN), D_MODEL ** -0.5),
        "gla_conv_w": nrm(ks[16], (N_B, CONV_K, CONV_K, QKV_B), 1.0 / CONV_K),
        "gla_w_gate": nrm(ks[17], (N_B, 2, GLA_RANK, KT_B), GLA_RANK ** -0.5),
        "gla_b_gate": nrm(ks[18], (N_B, 2, KT_B), 0.1),
        "gla_gnorm_w": 1.0 + nrm(ks[19], (N_B, DV_B), 0.02),
        "gla_w_out": nrm(ks[20], (N_B, DVT_B, D_MODEL), DVT_B ** -0.5),
        "peer_w_q": nrm(ks[21], (DEPTH, D_MODEL, P_HEADS * P_QDIM), D_MODEL ** -0.5),
        "peer_keys": nrm(ks[22], (DEPTH, P_HEADS, 2, N_KEYS, P_HALF), P_HALF ** -0.5),
        "peer_u": nrm(ks[23], (DEPTH, N_EXPERTS, D_MODEL), D_MODEL ** -0.5),
        "peer_v": nrm(ks[24], (DEPTH, N_EXPERTS, D_MODEL), 0.5),
    }


def reference(x_prompt, x_sample, state_hgrn, state_gla, c, c_ctx, ada_w, ada_b,
              norm1_w, norm2_w, final_norm_w, hgrn_w_in, hgrn_lb, hgrn_gnorm_w,
              hgrn_w_out, gla_w_in, gla_conv_w, gla_w_gate, gla_b_gate, gla_gnorm_w,
              gla_w_out, peer_w_q, peer_keys, peer_u, peer_v):
    rows = x_sample.shape[1] // GRID_W
    bp = x_prompt.shape[0]
    p = jax.nn.softmax(hgrn_lb.astype(f32), axis=0)
    lower_bounds = jnp.cumsum(p, axis=0) - p[0:1]

    xp, xs = x_prompt, x_sample
    new_hgrn, new_gla = [], []
    for l in range(DEPTH):
        j = l // 2
        sh1p, sc1p, g1p, sh2p, sc2p, g2p = _adaln(c_ctx[None, :], ada_w[l], ada_b[l])
        sh1s, sc1s, g1s, sh2s, sc2s, g2s = _adaln(c, ada_w[l], ada_b[l])
        hp = _modulate(_rms_norm(xp, norm1_w[l]), sh1p, sc1p)
        hs = _modulate(_rms_norm(xs, norm1_w[l]), sh1s, sc1s)
        if l % 2 == 0:
            z0 = jnp.zeros((bp, H_A, DK_A, DV_A), f32)
            mp, sf, sb = _hgrn2_mixer(hp, hgrn_w_in[j], lower_bounds[j], hgrn_gnorm_w[j],
                                      hgrn_w_out[j], z0, z0)
            ms, _, _ = _hgrn2_mixer(hs, hgrn_w_in[j], lower_bounds[j], hgrn_gnorm_w[j],
                                    hgrn_w_out[j], state_hgrn[:, j, 0], state_hgrn[:, j, 1])
            new_hgrn.append(jnp.stack([sf, sb], axis=1))
        else:
            z0 = jnp.zeros((bp, H_B, DK_B, DV_B), f32)
            mp, sf, sb = _gla_mixer(hp, 1, gla_w_in[j], gla_conv_w[j], gla_w_gate[j],
                                    gla_b_gate[j], gla_gnorm_w[j], gla_w_out[j], z0, z0)
            ms, _, _ = _gla_mixer(hs, rows, gla_w_in[j], gla_conv_w[j], gla_w_gate[j],
                                  gla_b_gate[j], gla_gnorm_w[j], gla_w_out[j],
                                  state_gla[:, j, 0], state_gla[:, j, 1])
            new_gla.append(jnp.stack([sf, sb], axis=1))
        xp = xp + g1p[:, None, :] * mp
        xs = xs + g1s[:, None, :] * ms
        hp = _modulate(_rms_norm(xp, norm2_w[l]), sh2p, sc2p)
        hs = _modulate(_rms_norm(xs, norm2_w[l]), sh2s, sc2s)
        xp = xp + g2p[:, None, :] * _peer(hp, peer_w_q[l], peer_keys[l], peer_u[l], peer_v[l])
        xs = xs + g2s[:, None, :] * _peer(hs, peer_w_q[l], peer_keys[l], peer_u[l], peer_v[l])

    y_prompt = _rms_norm(xp, final_norm_w)
    y_sample = _rms_norm(xs, final_norm_w)
    new_state_hgrn = jnp.stack(new_hgrn, axis=1).astype(x_prompt.dtype)
    new_state_gla = jnp.stack(new_gla, axis=1).astype(x_prompt.dtype)
    return (y_prompt, y_sample, new_state_hgrn, new_state_gla)
```

```python
import functools

import jax
import jax.numpy as jnp
from jax import lax
from jax.experimental import pallas as pl
from jax.experimental.pallas import tpu as pltpu

F32 = jnp.float32
BF16 = jnp.bfloat16

D_MODEL = 1024
DEPTH = 4
GRID_W = 64
EPS = 1e-6
F_MIN = 1e-20
H_A, DK_A, DV_A = 8, 128, 128
H_B, DK_B, DV_B = 4, 128, 256
QKV_B = 2048
GLA_RANK = 16
GATE_NORM = 16.0
P_HEADS, N_KEYS, P_TOPK, P_HALF = 8, 128, 16, 128
N_SEL = P_HEADS * P_TOPK

V7X_VMEM_LIMIT = 48 * 1024 * 1024
LANE = 128
ROW_TILE = 512
CHUNK = 64
SUB = 16
MOD_ROWS = 16
NEG_INF = float("-inf")


def _cparams(sem):
    return pltpu.CompilerParams(dimension_semantics=sem, vmem_limit_bytes=V7X_VMEM_LIMIT)


def _silu(x):
    return x * jax.nn.sigmoid(x)


def _gelu(x):
    return 0.5 * x * (1.0 + lax.erf(x * (2.0 ** -0.5)))


def _dot(a, b):
    return jnp.dot(a, b, preferred_element_type=F32)


def _dot_nt(a, b):
    return lax.dot_general(a, b, (((1,), (1,)), ((), ())), preferred_element_type=F32)


def _dot_tn(a, b):
    return lax.dot_general(a, b, (((0,), (0,)), ((), ())), preferred_element_type=F32)


def _dot_01(m01, x):
    x1 = x.astype(BF16)
    r1 = x - x1.astype(F32)
    x2 = r1.astype(BF16)
    x3 = (r1 - x2.astype(F32)).astype(BF16)
    return _dot(m01, x1) + _dot(m01, x2) + _dot(m01, x3)


def _adaln_kernel(c_ref, w_ref, b_ref, o_ref):
    o_ref[0] = _dot(_silu(c_ref[...]), w_ref[0]) + b_ref[0]


def _adaln(cvec, ada_w, ada_b):
    tn = 1536
    n = 6 * D_MODEL
    return pl.pallas_call(
        _adaln_kernel,
        grid=(DEPTH, n // tn),
        in_specs=[
            pl.BlockSpec((MOD_ROWS, D_MODEL), lambda l, j: (0, 0)),
            pl.BlockSpec((1, D_MODEL, tn), lambda l, j: (l, 0, j)),
            pl.BlockSpec((1, 1, tn), lambda l, j: (l, 0, j)),
        ],
        out_specs=pl.BlockSpec((1, MOD_ROWS, tn), lambda l, j: (l, 0, j)),
        out_shape=jax.ShapeDtypeStruct((DEPTH, MOD_ROWS, n), F32),
        compiler_params=_cparams(("parallel", "parallel")),
        name="adaln",
    )(cvec, ada_w, ada_b.reshape(DEPTH, 1, n))


class _Stream:
    def __init__(self, t_prompt, seq_sample, tm):
        assert t_prompt % tm == 0 and seq_sample % tm == 0
        self.tm = tm
        self.n_prompt_blocks = t_prompt // tm
        self.blocks_per_sample = seq_sample // tm

    def mod_index(self, layer, k):
        npb, bps = self.n_prompt_blocks, self.blocks_per_sample

        def f(i, *_):
            row = jnp.where(i < npb, 0, 1 + (i - npb) // bps)
            return ((layer * MOD_ROWS + row) * 6 + k, 0, 0)

        return f


def _norm_mod(x, nw, shift, scale):
    ms = jnp.mean(x * x, axis=-1, keepdims=True)
    return (x * lax.rsqrt(ms + EPS) * nw) * (1.0 + scale) + shift


def _nmm_kernel(emit_h, x_ref, nw_ref, sh_ref, sc_ref, w_ref, *rest):
    if emit_h:
        o_ref, h_ref, hs_ref = rest
    else:
        o_ref, hs_ref = rest

    @pl.when(pl.program_id(1) == 0)
    def _():
        h = _norm_mod(x_ref[...], nw_ref[...], sh_ref[0], sc_ref[0])
        hs_ref[...] = h.astype(BF16)
        if emit_h:
            h_ref[...] = h

    o_ref[...] = _dot(hs_ref[...], w_ref[...])


def _norm_mod_matmul(x, nw, mod3, stream, layer, k_shift, k_scale, w_bf16, tn, emit_h=False):
    t, d = x.shape
    n = w_bf16.shape[1]
    tm = stream.tm
    assert n % tn == 0
    out_shape = [jax.ShapeDtypeStruct((t, n), F32)]
    out_specs = [pl.BlockSpec((tm, tn), lambda i, j: (i, j))]
    if emit_h:
        out_shape.append(jax.ShapeDtypeStruct((t, d), F32))
        out_specs.append(pl.BlockSpec((tm, d), lambda i, j: (i, 0)))
    res = pl.pallas_call(
        functools.partial(_nmm_kernel, emit_h),
        grid=(t // tm, n // tn),
        in_specs=[
            pl.BlockSpec((tm, d), lambda i, j: (i, 0)),
            pl.BlockSpec((1, d), lambda i, j: (0, 0)),
            pl.BlockSpec((1, 1, d), stream.mod_index(layer, k_shift)),
            pl.BlockSpec((1, 1, d), stream.mod_index(layer, k_scale)),
            pl.BlockSpec((d, tn), lambda i, j: (0, j)),
        ],
        out_specs=out_specs,
        out_shape=out_shape,
        scratch_shapes=[pltpu.VMEM((tm, d), BF16)],
        compiler_params=_cparams(("parallel", "arbitrary")),
        name="norm_mod_matmul",
    )(x, nw.reshape(1, d), mod3, mod3, w_bf16)
    return res if emit_h else res[0]


def _scan_direction(q, k, v, g, st_ref, sidx, rev):
    c = CHUNK
    row = lax.broadcasted_iota(jnp.int32, (c, c), 0)
    col = lax.broadcasted_iota(jnp.int32, (c, c), 1)
    cmat = jnp.where((col >= row) if rev else (col <= row), 1.0, 0.0).astype(BF16)
    cs = _dot_01(cmat, g)
    st = st_ref[sidx]
    o = _dot_nt(q * jnp.exp(cs), st)
    edge = cs[0:1] if rev else cs[c - 1:c]
    st_ref[sidx] = st * jnp.exp(edge) + _dot_tn(v, k * jnp.exp(edge - cs))

    srow = lax.broadcasted_iota(jnp.int32, (SUB, c), 0)
    scol = lax.broadcasted_iota(jnp.int32, (SUB, c), 1)
    att_rows = []
    for blk in range(c // SUB):
        r0 = blk * SUB
        q_i, k_i, cs_i = q[r0:r0 + SUB], k[r0:r0 + SUB], cs[r0:r0 + SUB]
        has_off = (blk < c // SUB - 1) if rev else (blk > 0)
        if has_off:
            ref = cs[r0 + SUB:r0 + SUB + 1] if rev else cs[r0 - 1:r0]
            w_i = cs_i - ref
            kt = k * jnp.exp(jnp.minimum(ref - cs, 0.0))
            off = _dot_nt(q_i * jnp.exp(w_i), kt)
            off_mask = (scol >= r0 + SUB) if rev else (scol < r0)
            a_i = jnp.where(off_mask, off, 0.0)
        else:
            w_i = cs_i
            a_i = jnp.zeros((SUB, c), F32)
        for s in range(SUB):
            e = jnp.exp(jnp.minimum(w_i - w_i[s:s + 1], 0.0))
            colv = jnp.sum(q_i * e * k_i[s:s + 1], axis=1, keepdims=True)
            keep = (srow <= s) if rev else (srow >= s)
            a_i = jnp.where((scol == r0 + s) & keep, colv, a_i)
        att_rows.append(a_i)
    att = jnp.concatenate(att_rows, axis=0)
    return o + _dot(att, v)


def _log_sigmoid(x):
    return jnp.minimum(x, 0.0) - jnp.log(1.0 + jnp.exp(-jnp.abs(x)))


def _scan_kernel(mode, zero_init, *refs):
    if mode == "hgrn":
        (qf, vf, ff, qb, vb, fb, lbf, lbb), rest = refs[:8], refs[8:]
    else:
        (qf, kf, vf, lowf, qb, kb, vb, lowb, wgf, wgb, bgf, bgb), rest = refs[:12], refs[12:]
    if not zero_init:
        (s0f, s0b), rest = rest[:2], rest[2:]
    of_ref, ob_ref, sf_ref, sb_ref, st_ref = rest
    ci = pl.program_id(2)

    @pl.when(ci == 0)
    def _():
        if zero_init:
            st_ref[...] = jnp.zeros_like(st_ref)
        else:
            st_ref[0] = s0f[0, 0]
            st_ref[1] = s0b[0, 0]

    def prep(q_ref, k_ref, f_ref, lb_ref, low_ref, wg_ref, bg_ref):
        if mode == "hgrn":
            lb = lb_ref[...]
            fp = f_ref[...]
            f = lb + (1.0 - lb) * jax.nn.sigmoid(fp)
            g = jnp.log(jnp.maximum(f, F_MIN))
            k = (1.0 - lb) * jax.nn.sigmoid(-fp)
            q = _silu(q_ref[...])
        else:
            gk = _dot(low_ref[...], wg_ref[...]) + bg_ref[...]
            g = _log_sigmoid(gk) * (1.0 / GATE_NORM)
            k = k_ref[...]
            q = q_ref[...] * (DK_B ** -0.5)
        return q, k, g

    if mode == "hgrn":
        q, k, g = prep(qf, None, ff, lbf, None, None, None)
        of_ref[...] = _scan_direction(q, k, vf[...], g, st_ref, 0, False)
        q, k, g = prep(qb, None, fb, lbb, None, None, None)
        ob_ref[...] = _scan_direction(q, k, vb[...], g, st_ref, 1, True)
    else:
        q, k, g = prep(qf, kf, None, None, lowf, wgf, bgf)
        of_ref[...] = _scan_direction(q, k, vf[...], g, st_ref, 0, False)
        q, k, g = prep(qb, kb, None, None, lowb, wgb, bgb)
        ob_ref[...] = _scan_direction(q, k, vb[...], g, st_ref, 1, True)

    @pl.when(ci == pl.num_programs(2) - 1)
    def _():
        sf_ref[0, 0] = st_ref[0]
        sb_ref[0, 0] = st_ref[1]


def _scan(mode, arrays, consts, init_states, batch, seq, heads, dk, dv):
    n = seq // CHUNK
    zero_init = init_states is None

    def chunk_of(c, dirn):
        return c if dirn == 0 else n - 1 - c

    in_specs, operands = [], []
    for dirn in (0, 1):
        for arr, width, row0, cols, per_head in arrays:
            assert row0 % CHUNK == 0

            def imap(b, h, c, dirn=dirn, rb0=row0 // CHUNK, col0=cols[dirn], per_head=per_head):
                return (rb0 + b * n + chunk_of(c, dirn), col0 + (h if per_head else 0))

            in_specs.append(pl.BlockSpec((CHUNK, width), imap))
            operands.append(arr)
    for arr, shape in consts:
        in_specs.append(pl.BlockSpec(shape, lambda b, h, c: (0, h)))
        operands.append(arr)
    if not zero_init:
        for s0 in init_states:
            in_specs.append(pl.BlockSpec((1, 1, dv, dk), lambda b, h, c: (b, h, 0, 0)))
            operands.append(s0)
    o_specs = [pl.BlockSpec((CHUNK, dv), lambda b, h, c, dirn=dirn: (b * n + chunk_of(c, dirn), h)) for dirn in (0, 1)]
    s_spec = pl.BlockSpec((1, 1, dv, dk), lambda b, h, c: (b, h, 0, 0))
    return pl.pallas_call(
        functools.partial(_scan_kernel, mode, zero_init),
        grid=(batch, heads, n),
        in_specs=in_specs,
        out_specs=o_specs + [s_spec, s_spec],
        out_shape=[
            jax.ShapeDtypeStruct((batch * seq, heads * dv), F32),
            jax.ShapeDtypeStruct((batch * seq, heads * dv), F32),
            jax.ShapeDtypeStruct((batch, heads, dv, dk), F32),
            jax.ShapeDtypeStruct((batch, heads, dv, dk), F32),
        ],
        scratch_shapes=[pltpu.VMEM((2, dv, dk), F32)],
        compiler_params=_cparams(("parallel", "parallel", "arbitrary")),
        name="scan_" + mode,
    )(*operands)


def _hgrn_scan(z, lb, init_states, row0, batch, seq):
    nh = H_A
    arrays = [
        (z, DK_A, row0, (0, 0), True),
        (z, DV_A, row0, (nh, nh), True),
        (z, DK_A, row0, (2 * nh, 3 * nh), True),
    ]
    consts = [(lb[0:1], (1, DK_A)), (lb[1:2], (1, DK_A))]
    return _scan("hgrn", arrays, consts, init_states, batch, seq, H_A, DK_A, DV_A)


def _gla_scan(y, z, wg, bg, init_states, row0, batch, seq, low_col_block):
    v0 = 2 * H_B * DK_B // DV_B
    arrays = [
        (y, DK_B, 0, (0, 0), True),
        (y, DK_B, 0, (H_B, H_B), True),
        (y, DV_B, 0, (v0, v0), True),
        (z, LANE, row0, (low_col_block, low_col_block), False),
    ]
    consts = [(wg[0], (LANE, DK_B)), (wg[1], (LANE, DK_B)), (bg[0:1], (1, DK_B)), (bg[1:2], (1, DK_B))]
    return _scan("gla", arrays, consts, init_states, batch, seq, H_B, DK_B, DV_B)


def _conv_kernel(rows, width, x_ref, w_ref, o_ref):
    seq = rows * width
    x = x_ref[...]
    assert width & (width - 1) == 0
    l = lax.broadcasted_iota(jnp.int32, (seq, 1), 0)
    r, c = l >> (width.bit_length() - 1), l & (width - 1)
    acc = jnp.zeros_like(x)
    for di in range(3):
        if rows == 1 and di != 1:
            continue
        for dj in range(3):
            k = (di - 1) * width + (dj - 1)
            xs = x if k == 0 else pltpu.roll(x, (-k) % seq, 0)
            ok = (r + (di - 1) >= 0) & (r + (di - 1) < rows) & (c + (dj - 1) >= 0) & (c + (dj - 1) < width)
            acc = acc + jnp.where(ok, xs, 0.0) * w_ref[3 * di + dj:3 * di + dj + 1, :]
    o_ref[...] = _silu(acc)


def _conv_silu(z, conv_w9, row0, batch, rows, width, tc=256):
    seq = rows * width
    sb0 = row0 // seq
    assert row0 % seq == 0
    return pl.pallas_call(
        functools.partial(_conv_kernel, rows, width),
        grid=(batch, QKV_B // tc),
        in_specs=[
            pl.BlockSpec((seq, tc), lambda b, j: (sb0 + b, j)),
            pl.BlockSpec((9, tc), lambda b, j: (0, j)),
        ],
        out_specs=pl.BlockSpec((seq, tc), lambda b, j: (b, j)),
        out_shape=jax.ShapeDtypeStruct((batch * seq, QKV_B), F32),
        compiler_params=_cparams(("parallel", "parallel")),
        name="conv_silu",
    )(z, conv_w9)


def _outproj_kernel(heads, dvh, of_ref, ob_ref, og_ref, gw_ref, w_ref, x_ref, g_ref, o_ref):
    o = of_ref[...] + ob_ref[...]
    og = og_ref[...]
    gw = gw_ref[...]
    parts = []
    for h in range(heads):
        oh = o[:, h * dvh:(h + 1) * dvh]
        ms = jnp.mean(oh * oh, axis=-1, keepdims=True)
        parts.append(oh * lax.rsqrt(ms + EPS) * gw)
    y = jnp.concatenate(parts, axis=1) * _silu(og)
    o_ref[...] = x_ref[...] + g_ref[0] * _dot(y.astype(BF16), w_ref[...])


def _outproj(o_f, o_b, z, og_col_block, gnorm_w, w_out_bf16, x, mod3, stream, layer, heads, dvh):
    t, d = x.shape
    tm = 256
    st = _Stream(stream.n_prompt_blocks * stream.tm, stream.blocks_per_sample * stream.tm, tm)
    return pl.pallas_call(
        functools.partial(_outproj_kernel, heads, dvh),
        grid=(t // tm,),
        in_specs=[
            pl.BlockSpec((tm, d), lambda i: (i, 0)),
            pl.BlockSpec((tm, d), lambda i: (i, 0)),
            pl.BlockSpec((tm, d), lambda i: (i, og_col_block)),
            pl.BlockSpec((1, dvh), lambda i: (0, 0)),
            pl.BlockSpec((d, d), lambda i: (0, 0)),
            pl.BlockSpec((tm, d), lambda i: (i, 0)),
            pl.BlockSpec((1, 1, d), st.mod_index(layer, 2)),
        ],
        out_specs=pl.BlockSpec((tm, d), lambda i: (i, 0)),
        out_shape=jax.ShapeDtypeStruct((t, d), F32),
        compiler_params=_cparams(("parallel",)),
        name="outproj",
    )(o_f, o_b, z, gnorm_w.reshape(1, dvh), w_out_bf16, x, mod3)


def _top_rows(s, k):
    n, tm = s.shape
    rid = lax.broadcasted_iota(jnp.int32, (n, tm), 0).astype(F32)
    kid = lax.broadcasted_iota(jnp.int32, (k, tm), 0)

    def body(j, carry):
        s, vals, ids = carry
        m = jnp.max(s, axis=0, keepdims=True)
        pos = jnp.min(jnp.where(s == m, rid, float(n)), axis=0, keepdims=True)
        vals = jnp.where(kid == j, m, vals)
        ids = jnp.where(kid == j, pos, ids)
        s = jnp.where(rid == pos, NEG_INF, s)
        return s, vals, ids

    _, vals, ids = lax.fori_loop(0, k, body, (s, jnp.zeros((k, tm), F32), jnp.zeros((k, tm), F32)))
    return vals, ids


_PAIR_COUNTS = [P_TOPK // (a + 1) for a in range(P_TOPK)]
_N_PAIRS = sum(_PAIR_COUNTS)
_N_PAIR_ROWS = -(-_N_PAIRS // 8) * 8


def _route_kernel(q_ref, keys_ref, idx_ref, gate_ref, cand_ref, cid_ref):
    tm = q_ref.shape[0]
    vals, ids = [], []
    for p in range(2):
        s = _dot_nt(keys_ref[0, p], q_ref[:, p * P_HALF:(p + 1) * P_HALF])
        v, i = _top_rows(s, P_TOPK)
        vals.append(v)
        ids.append(i)
    cand_ref[...] = jnp.full(cand_ref.shape, NEG_INF, F32)
    cid_ref[...] = jnp.zeros(cid_ref.shape, F32)
    r0 = 0
    for a, nb in enumerate(_PAIR_COUNTS):
        cand_ref[r0:r0 + nb, :] = vals[0][a:a + 1] + vals[1][0:nb]
        cid_ref[r0:r0 + nb, :] = ids[0][a:a + 1] * float(N_KEYS) + ids[1][0:nb]
        r0 += nb
    cand = cand_ref[...]
    cid = cid_ref[...]
    n = cand.shape[0]
    rid = lax.broadcasted_iota(jnp.int32, (n, tm), 0).astype(F32)
    kid = lax.broadcasted_iota(jnp.int32, (P_TOPK, tm), 0)

    def body(j, carry):
        cand, top, idx = carry
        m = jnp.max(cand, axis=0, keepdims=True)
        pos = jnp.min(jnp.where(cand == m, rid, float(n)), axis=0, keepdims=True)
        sel = rid == pos
        e = jnp.sum(jnp.where(sel, cid, 0.0), axis=0, keepdims=True)
        top = jnp.where(kid == j, m, top)
        idx = jnp.where(kid == j, e, idx)
        return jnp.where(sel, NEG_INF, cand), top, idx

    _, top, idx = lax.fori_loop(
        0, P_TOPK, body, (cand, jnp.zeros((P_TOPK, tm), F32), jnp.zeros((P_TOPK, tm), F32)))
    ex = jnp.exp(top - top[0:1])
    gate_ref[...] = ex / jnp.sum(ex, axis=0, keepdims=True)
    idx_ref[...] = idx.astype(jnp.int32)


def _route(q, keys):
    t = q.shape[0]
    tm = 256
    return pl.pallas_call(
        _route_kernel,
        grid=(t // tm, P_HEADS),
        in_specs=[
            pl.BlockSpec((tm, 2 * P_HALF), lambda i, h: (i, h)),
            pl.BlockSpec((1, 2, N_KEYS, P_HALF), lambda i, h: (h, 0, 0, 0)),
        ],
        out_specs=[
            pl.BlockSpec((P_TOPK, tm), lambda i, h: (h, i)),
            pl.BlockSpec((P_TOPK, tm), lambda i, h: (h, i)),
        ],
        out_shape=[
            jax.ShapeDtypeStruct((N_SEL, t), jnp.int32),
            jax.ShapeDtypeStruct((N_SEL, t), F32),
        ],
        scratch_shapes=[pltpu.VMEM((_N_PAIR_ROWS, tm), F32), pltpu.VMEM((_N_PAIR_ROWS, tm), F32)],
        compiler_params=_cparams(("parallel", "parallel")),
        name="peer_route",
    )(q, keys)


PEER_TB = 64


def _peer_kernel(idx_ref, h_ref, gate_ref, x_ref, g2_ref, uv_hbm, o_ref, buf, sem):
    d = D_MODEL

    def row_copy(t, e, slot):
        return pltpu.make_async_copy(
            uv_hbm.at[pl.ds(idx_ref[t, e], 1), :], buf.at[slot, pl.ds(e, 1), :], sem.at[slot])

    def start_token(t, slot):
        def issue(e, _):
            row_copy(t, e, slot).start()
            return 0
        lax.fori_loop(0, N_SEL, issue, 0)

    def wait_token(slot):
        pltpu.make_async_copy(uv_hbm.at[pl.ds(0, N_SEL), :], buf.at[slot], sem.at[slot]).wait()

    eye = lax.broadcasted_iota(jnp.int32, (N_SEL, N_SEL), 0) == lax.broadcasted_iota(jnp.int32, (N_SEL, N_SEL), 1)
    g2 = g2_ref[0]
    start_token(0, 0)

    def body(t, _):
        slot = t % 2

        @pl.when(t + 1 < PEER_TB)
        def _():
            start_token(t + 1, 1 - slot)

        wait_token(slot)
        u = buf[slot, :, 0:d]
        v = buf[slot, :, d:2 * d]
        hrow = h_ref[pl.ds(t, 1), :]
        s = jnp.sum(u * hrow, axis=1, keepdims=True)
        gcol = jnp.sum(jnp.where(eye, gate_ref[pl.ds(t, 1), :], 0.0), axis=1, keepdims=True)
        coef = gcol * _gelu(s)
        out = jnp.sum(v * coef, axis=0, keepdims=True)
        o_ref[pl.ds(t, 1), :] = x_ref[pl.ds(t, 1), :] + g2 * out
        return 0

    lax.fori_loop(0, PEER_TB, body, 0)


def _peer(idx_t, gate_t, h, x, mod3, stream, layer, uv):
    t, d = x.shape
    tb = PEER_TB
    st = _Stream(stream.n_prompt_blocks * stream.tm, stream.blocks_per_sample * stream.tm, tb)
    return pl.pallas_call(
        _peer_kernel,
        grid=(t // tb,),
        in_specs=[
            pl.BlockSpec((tb, N_SEL), lambda i: (i, 0), memory_space=pltpu.SMEM),
            pl.BlockSpec((tb, d), lambda i: (i, 0)),
            pl.BlockSpec((tb, N_SEL), lambda i: (i, 0)),
            pl.BlockSpec((tb, d), lambda i: (i, 0)),
            pl.BlockSpec((1, 1, d), st.mod_index(layer, 5)),
            pl.BlockSpec(memory_space=pl.ANY),
        ],
        out_specs=pl.BlockSpec((tb, d), lambda i: (i, 0)),
        out_shape=jax.ShapeDtypeStruct((t, d), F32),
        scratch_shapes=[pltpu.VMEM((2, N_SEL, 2 * d), F32), pltpu.SemaphoreType.DMA((2,))],
        compiler_params=_cparams(("arbitrary",)),
        name="peer_experts",
    )(idx_t, h, gate_t, x, mod3, uv)


def _rms_kernel(x_ref, w_ref, o_ref):
    x = x_ref[...]
    ms = jnp.mean(x * x, axis=-1, keepdims=True)
    o_ref[...] = x * lax.rsqrt(ms + EPS) * w_ref[...]


def _rms_norm(x, w):
    t, d = x.shape
    tm = 512
    return pl.pallas_call(
        _rms_kernel,
        grid=(t // tm,),
        in_specs=[pl.BlockSpec((tm, d), lambda i: (i, 0)), pl.BlockSpec((1, d), lambda i: (0, 0))],
        out_specs=pl.BlockSpec((tm, d), lambda i: (i, 0)),
        out_shape=jax.ShapeDtypeStruct((t, d), F32),
        compiler_params=_cparams(("parallel",)),
        name="final_norm",
    )(x, w.reshape(1, d))


def kernel(x_prompt, x_sample, state_hgrn, state_gla, c, c_ctx, ada_w, ada_b, norm1_w, norm2_w, final_norm_w, hgrn_w_in, hgrn_lb, hgrn_gnorm_w, hgrn_w_out, gla_w_in, gla_conv_w, gla_w_gate, gla_b_gate, gla_gnorm_w, gla_w_out, peer_w_q, peer_keys, peer_u, peer_v):
    bp, lp, d = x_prompt.shape
    bs, ls, _ = x_sample.shape
    t_p, t_s = bp * lp, bs * ls
    rows_s = ls // GRID_W
    stream = _Stream(t_p, ls, ROW_TILE)

    x = jnp.concatenate([x_prompt.reshape(t_p, d), x_sample.reshape(t_s, d)], axis=0)

    cvec = jnp.concatenate([c_ctx[None, :], c, jnp.zeros((MOD_ROWS - 1 - bs, d), F32)], axis=0)
    mod3 = _adaln(cvec, ada_w, ada_b).reshape(DEPTH * MOD_ROWS * 6, 1, d)

    p = jax.nn.softmax(hgrn_lb.astype(F32), axis=0)
    lower_bounds = jnp.cumsum(p, axis=0) - p[0:1]

    gla_n = gla_w_in.shape[2]
    gla_np = -(-gla_n // LANE) * LANE
    low_col_block = (QKV_B + D_MODEL) // LANE
    n_b = gla_w_in.shape[0]
    wg_pad = jnp.zeros((n_b, 2, LANE, H_B * DK_B), F32)
    wg_pad = wg_pad.at[:, 0, 0:GLA_RANK].set(gla_w_gate[:, 0]).at[:, 1, GLA_RANK:2 * GLA_RANK].set(gla_w_gate[:, 1])

    new_hgrn, new_gla = [], []
    for l in range(DEPTH):
        j = l // 2
        if l % 2 == 0:
            z = _norm_mod_matmul(x, norm1_w[l], mod3, stream, l, 0, 1, hgrn_w_in[j].astype(BF16), 512)
            lb = lower_bounds[j]
            of_p, ob_p, sf, sb = _hgrn_scan(z, lb, None, 0, bp, lp)
            init = (jnp.swapaxes(state_hgrn[:, j, 0], -1, -2), jnp.swapaxes(state_hgrn[:, j, 1], -1, -2))
            of_s, ob_s, _, _ = _hgrn_scan(z, lb, init, t_p, bs, ls)
            new_hgrn.append(jnp.stack([jnp.swapaxes(sf, -1, -2), jnp.swapaxes(sb, -1, -2)], axis=1))
            heads, dvh, og_blk = H_A, DV_A, 4
            gnw, w_out = hgrn_gnorm_w[j], hgrn_w_out[j]
        else:
            w_in = jnp.pad(gla_w_in[j], ((0, 0), (0, gla_np - gla_n))).astype(BF16)
            z = _norm_mod_matmul(x, norm1_w[l], mod3, stream, l, 0, 1, w_in, gla_np // 5 if gla_np % 5 == 0 else LANE)
            cw = gla_conv_w[j].reshape(9, QKV_B)
            y_p = _conv_silu(z, cw, 0, bp, 1, lp)
            y_s = _conv_silu(z, cw, t_p, bs, rows_s, GRID_W)
            wg, bg = wg_pad[j], gla_b_gate[j]
            of_p, ob_p, sf, sb = _gla_scan(y_p, z, wg, bg, None, 0, bp, lp, low_col_block)
            init = (jnp.swapaxes(state_gla[:, j, 0], -1, -2), jnp.swapaxes(state_gla[:, j, 1], -1, -2))
            of_s, ob_s, _, _ = _gla_scan(y_s, z, wg, bg, init, t_p, bs, ls, low_col_block)
            new_gla.append(jnp.stack([jnp.swapaxes(sf, -1, -2), jnp.swapaxes(sb, -1, -2)], axis=1))
            heads, dvh, og_blk = H_B, DV_B, 2
            gnw, w_out = gla_gnorm_w[j], gla_w_out[j]
        o_f = jnp.concatenate([of_p, of_s], axis=0)
        o_b = jnp.concatenate([ob_p, ob_s], axis=0)
        x = _outproj(o_f, o_b, z, og_blk, gnw, w_out.astype(BF16), x, mod3, stream, l, heads, dvh)

        q, h = _norm_mod_matmul(x, norm2_w[l], mod3, stream, l, 3, 4, peer_w_q[l].astype(BF16), 512, emit_h=True)
        idx, gates = _route(q, peer_keys[l])
        uv = jnp.concatenate([peer_u[l], peer_v[l]], axis=1)
        x = _peer(idx.T, gates.T, h, x, mod3, stream, l, uv)

    y = _rms_norm(x, final_norm_w)
    y_prompt = y[:t_p].reshape(bp, lp, d)
    y_sample = y[t_p:].reshape(bs, ls, d)
    return (y_prompt, y_sample, jnp.stack(new_hgrn, axis=1), jnp.stack(new_gla, axis=1))
```

```python
import functools

import jax
import jax.numpy as jnp
from jax import lax
from jax.experimental import pallas as pl
from jax.experimental.pallas import tpu as pltpu

F32 = jnp.float32
BF16 = jnp.bfloat16

D_MODEL = 1024
DEPTH = 4
GRID_W = 64
EPS = 1e-6
F_MIN = 1e-20
H_A, DK_A, DV_A = 8, 128, 128
H_B, DK_B, DV_B = 4, 128, 256
QKV_B = 2048
GLA_RANK = 16
GATE_NORM = 16.0
P_HEADS, N_KEYS, P_TOPK, P_HALF = 8, 128, 16, 128
N_SEL = P_HEADS * P_TOPK

V7X_VMEM_LIMIT = 48 * 1024 * 1024
LANE = 128
ROW_TILE = 512
CHUNK = 64
SUB = 16
MOD_ROWS = 16
NEG_INF = float("-inf")


def _cparams(sem):
    return pltpu.CompilerParams(dimension_semantics=sem, vmem_limit_bytes=V7X_VMEM_LIMIT)


def _silu(x):
    return x * jax.nn.sigmoid(x)


def _gelu(x):
    return 0.5 * x * (1.0 + lax.erf(x * (2.0 ** -0.5)))


def _dot(a, b):
    return jnp.dot(a, b, preferred_element_type=F32)


def _dot_nt(a, b):
    return lax.dot_general(a, b, (((1,), (1,)), ((), ())), preferred_element_type=F32)


def _dot_tn(a, b):
    return lax.dot_general(a, b, (((0,), (0,)), ((), ())), preferred_element_type=F32)


def _dot_01(m01, x):
    x1 = x.astype(BF16)
    r1 = x - x1.astype(F32)
    x2 = r1.astype(BF16)
    x3 = (r1 - x2.astype(F32)).astype(BF16)
    return _dot(m01, x1) + _dot(m01, x2) + _dot(m01, x3)


def _adaln_kernel(c_ref, w_ref, b_ref, o_ref):
    o_ref[0] = _dot(_silu(c_ref[...]), w_ref[0]) + b_ref[0]


def _adaln(cvec, ada_w, ada_b):
    tn = 1536
    n = 6 * D_MODEL
    return pl.pallas_call(
        _adaln_kernel,
        grid=(DEPTH, n // tn),
        in_specs=[
            pl.BlockSpec((MOD_ROWS, D_MODEL), lambda l, j: (0, 0)),
            pl.BlockSpec((1, D_MODEL, tn), lambda l, j: (l, 0, j)),
            pl.BlockSpec((1, 1, tn), lambda l, j: (l, 0, j)),
        ],
        out_specs=pl.BlockSpec((1, MOD_ROWS, tn), lambda l, j: (l, 0, j)),
        out_shape=jax.ShapeDtypeStruct((DEPTH, MOD_ROWS, n), F32),
        compiler_params=_cparams(("parallel", "parallel")),
        name="adaln",
    )(cvec, ada_w, ada_b.reshape(DEPTH, 1, n))


class _Stream:
    def __init__(self, t_prompt, seq_sample, tm):
        assert t_prompt % tm == 0 and seq_sample % tm == 0
        self.tm = tm
        self.n_prompt_blocks = t_prompt // tm
        self.blocks_per_sample = seq_sample // tm

    def mod_index(self, layer, k):
        npb, bps = self.n_prompt_blocks, self.blocks_per_sample

        def f(i, *_):
            row = jnp.where(i < npb, 0, 1 + (i - npb) // bps)
            return ((layer * MOD_ROWS + row) * 6 + k, 0, 0)

        return f


def _norm_mod(x, nw, shift, scale):
    ms = jnp.mean(x * x, axis=-1, keepdims=True)
    return (x * lax.rsqrt(ms + EPS) * nw) * (1.0 + scale) + shift


def _nmm_kernel(emit_h, x_ref, nw_ref, sh_ref, sc_ref, w_ref, *rest):
    if emit_h:
        o_ref, h_ref, hs_ref = rest
    else:
        o_ref, hs_ref = rest

    @pl.when(pl.program_id(1) == 0)
    def _():
        h = _norm_mod(x_ref[...], nw_ref[...], sh_ref[0], sc_ref[0])
        hs_ref[...] = h.astype(BF16)
        if emit_h:
            h_ref[...] = h

    o_ref[...] = _dot(hs_ref[...], w_ref[...])


def _norm_mod_matmul(x, nw, mod3, stream, layer, k_shift, k_scale, w_bf16, tn, emit_h=False):
    t, d = x.shape
    n = w_bf16.shape[1]
    tm = stream.tm
    assert n % tn == 0
    out_shape = [jax.ShapeDtypeStruct((t, n), F32)]
    out_specs = [pl.BlockSpec((tm, tn), lambda i, j: (i, j))]
    if emit_h:
        out_shape.append(jax.ShapeDtypeStruct((t, d), F32))
        out_specs.append(pl.BlockSpec((tm, d), lambda i, j: (i, 0)))
    res = pl.pallas_call(
        functools.partial(_nmm_kernel, emit_h),
        grid=(t // tm, n // tn),
        in_specs=[
            pl.BlockSpec((tm, d), lambda i, j: (i, 0)),
            pl.BlockSpec((1, d), lambda i, j: (0, 0)),
            pl.BlockSpec((1, 1, d), stream.mod_index(layer, k_shift)),
            pl.BlockSpec((1, 1, d), stream.mod_index(layer, k_scale)),
            pl.BlockSpec((d, tn), lambda i, j: (0, j)),
        ],
        out_specs=out_specs,
        out_shape=out_shape,
        scratch_shapes=[pltpu.VMEM((tm, d), BF16)],
        compiler_params=_cparams(("parallel", "arbitrary")),
        name="norm_mod_matmul",
    )(x, nw.reshape(1, d), mod3, mod3, w_bf16)
    return res if emit_h else res[0]


def _scan_direction(q, k, v, g, st_ref, sidx, rev):
    c = CHUNK
    row = lax.broadcasted_iota(jnp.int32, (c, c), 0)
    col = lax.broadcasted_iota(jnp.int32, (c, c), 1)
    cmat = jnp.where((col >= row) if rev else (col <= row), 1.0, 0.0).astype(BF16)
    cs = _dot_01(cmat, g)
    st = st_ref[sidx]
    o = _dot_nt(q * jnp.exp(cs), st)
    edge = cs[0:1] if rev else cs[c - 1:c]
    st_ref[sidx] = st * jnp.exp(edge) + _dot_tn(v, k * jnp.exp(edge - cs))

    srow = lax.broadcasted_iota(jnp.int32, (SUB, c), 0)
    scol = lax.broadcasted_iota(jnp.int32, (SUB, c), 1)
    att_rows = []
    for blk in range(c // SUB):
        r0 = blk * SUB
        q_i, k_i, cs_i = q[r0:r0 + SUB], k[r0:r0 + SUB], cs[r0:r0 + SUB]
        has_off = (blk < c // SUB - 1) if rev else (blk > 0)
        if has_off:
            ref = cs[r0 + SUB:r0 + SUB + 1] if rev else cs[r0 - 1:r0]
            w_i = cs_i - ref
            kt = k * jnp.exp(jnp.minimum(ref - cs, 0.0))
            off = _dot_nt(q_i * jnp.exp(w_i), kt)
            off_mask = (scol >= r0 + SUB) if rev else (scol < r0)
            a_i = jnp.where(off_mask, off, 0.0)
        else:
            w_i = cs_i
            a_i = jnp.zeros((SUB, c), F32)
        for s in range(SUB):
            e = jnp.exp(jnp.minimum(w_i - w_i[s:s + 1], 0.0))
            colv = jnp.sum(q_i * e * k_i[s:s + 1], axis=1, keepdims=True)
            keep = (srow <= s) if rev else (srow >= s)
            a_i = jnp.where((scol == r0 + s) & keep, colv, a_i)
        att_rows.append(a_i)
    att = jnp.concatenate(att_rows, axis=0)
    return o + _dot(att, v)


def _log_sigmoid(x):
    return jnp.minimum(x, 0.0) - jnp.log(1.0 + jnp.exp(-jnp.abs(x)))


def _scan_kernel(mode, zero_init, *refs):
    if mode == "hgrn":
        (qf, vf, ff, qb, vb, fb, lbf, lbb), rest = refs[:8], refs[8:]
    else:
        (qf, kf, vf, lowf, qb, kb, vb, lowb, wgf, wgb, bgf, bgb), rest = refs[:12], refs[12:]
    if not zero_init:
        (s0f, s0b), rest = rest[:2], rest[2:]
    of_ref, ob_ref, sf_ref, sb_ref, st_ref = rest
    ci = pl.program_id(2)

    @pl.when(ci == 0)
    def _():
        if zero_init:
            st_ref[...] = jnp.zeros_like(st_ref)
        else:
            st_ref[0] = s0f[0, 0]
            st_ref[1] = s0b[0, 0]

    def prep(q_ref, k_ref, f_ref, lb_ref, low_ref, wg_ref, bg_ref):
        if mode == "hgrn":
            lb = lb_ref[...]
            fp = f_ref[...]
            f = lb + (1.0 - lb) * jax.nn.sigmoid(fp)
            g = jnp.log(jnp.maximum(f, F_MIN))
            k = (1.0 - lb) * jax.nn.sigmoid(-fp)
            q = _silu(q_ref[...])
        else:
            gk = _dot(low_ref[...], wg_ref[...]) + bg_ref[...]
            g = _log_sigmoid(gk) * (1.0 / GATE_NORM)
            k = k_ref[...]
            q = q_ref[...] * (DK_B ** -0.5)
        return q, k, g

    if mode == "hgrn":
        q, k, g = prep(qf, None, ff, lbf, None, None, None)
        of_ref[...] = _scan_direction(q, k, vf[...], g, st_ref, 0, False)
        q, k, g = prep(qb, None, fb, lbb, None, None, None)
        ob_ref[...] = _scan_direction(q, k, vb[...], g, st_ref, 1, True)
    else:
        q, k, g = prep(qf, kf, None, None, lowf, wgf, bgf)
        of_ref[...] = _scan_direction(q, k, vf[...], g, st_ref, 0, False)
        q, k, g = prep(qb, kb, None, None, lowb, wgb, bgb)
        ob_ref[...] = _scan_direction(q, k, vb[...], g, st_ref, 1, True)

    @pl.when(ci == pl.num_programs(2) - 1)
    def _():
        sf_ref[0, 0] = st_ref[0]
        sb_ref[0, 0] = st_ref[1]


def _scan(mode, arrays, consts, init_states, batch, seq, heads, dk, dv):
    n = seq // CHUNK
    zero_init = init_states is None

    def chunk_of(c, dirn):
        return c if dirn == 0 else n - 1 - c

    in_specs, operands = [], []
    for dirn in (0, 1):
        for arr, width, row0, cols, per_head in arrays:
            assert row0 % CHUNK == 0

            def imap(b, h, c, dirn=dirn, rb0=row0 // CHUNK, col0=cols[dirn], per_head=per_head):
                return (rb0 + b * n + chunk_of(c, dirn), col0 + (h if per_head else 0))

            in_specs.append(pl.BlockSpec((CHUNK, width), imap))
            operands.append(arr)
    for arr, shape in consts:
        in_specs.append(pl.BlockSpec(shape, lambda b, h, c: (0, h)))
        operands.append(arr)
    if not zero_init:
        for s0 in init_states:
            in_specs.append(pl.BlockSpec((1, 1, dv, dk), lambda b, h, c: (b, h, 0, 0)))
            operands.append(s0)
    o_specs = [pl.BlockSpec((CHUNK, dv), lambda b, h, c, dirn=dirn: (b * n + chunk_of(c, dirn), h)) for dirn in (0, 1)]
    s_spec = pl.BlockSpec((1, 1, dv, dk), lambda b, h, c: (b, h, 0, 0))
    return pl.pallas_call(
        functools.partial(_scan_kernel, mode, zero_init),
        grid=(batch, heads, n),
        in_specs=in_specs,
        out_specs=o_specs + [s_spec, s_spec],
        out_shape=[
            jax.ShapeDtypeStruct((batch * seq, heads * dv), F32),
            jax.ShapeDtypeStruct((batch * seq, heads * dv), F32),
            jax.ShapeDtypeStruct((batch, heads, dv, dk), F32),
            jax.ShapeDtypeStruct((batch, heads, dv, dk), F32),
        ],
        scratch_shapes=[pltpu.VMEM((2, dv, dk), F32)],
        compiler_params=_cparams(("parallel", "parallel", "arbitrary")),
        name="scan_" + mode,
    )(*operands)


def _hgrn_scan(z, lb, init_states, row0, batch, seq):
    nh = H_A
    arrays = [
        (z, DK_A, row0, (0, 0), True),
        (z, DV_A, row0, (nh, nh), True),
        (z, DK_A, row0, (2 * nh, 3 * nh), True),
    ]
    consts = [(lb[0:1], (1, DK_A)), (lb[1:2], (1, DK_A))]
    return _scan("hgrn", arrays, consts, init_states, batch, seq, H_A, DK_A, DV_A)


def _gla_scan(y, z, wg, bg, init_states, row0, batch, seq, low_col_block):
    v0 = 2 * H_B * DK_B // DV_B
    arrays = [
        (y, DK_B, 0, (0, 0), True),
        (y, DK_B, 0, (H_B, H_B), True),
        (y, DV_B, 0, (v0, v0), True),
        (z, LANE, row0, (low_col_block, low_col_block), False),
    ]
    consts = [(wg[0], (LANE, DK_B)), (wg[1], (LANE, DK_B)), (bg[0:1], (1, DK_B)), (bg[1:2], (1, DK_B))]
    return _scan("gla", arrays, consts, init_states, batch, seq, H_B, DK_B, DV_B)


def _conv_kernel(rows, width, x_ref, w_ref, o_ref):
    seq = rows * width
    x = x_ref[...]
    assert width & (width - 1) == 0
    l = lax.broadcasted_iota(jnp.int32, (seq, 1), 0)
    r, c = l >> (width.bit_length() - 1), l & (width - 1)
    acc = jnp.zeros_like(x)
    for di in range(3):
        if rows == 1 and di != 1:
            continue
        for dj in range(3):
            k = (di - 1) * width + (dj - 1)
            xs = x if k == 0 else pltpu.roll(x, (-k) % seq, 0)
            ok = (r + (di - 1) >= 0) & (r + (di - 1) < rows) & (c + (dj - 1) >= 0) & (c + (dj - 1) < width)
            acc = acc + jnp.where(ok, xs, 0.0) * w_ref[3 * di + dj:3 * di + dj + 1, :]
    o_ref[...] = _silu(acc)


def _conv_silu(z, conv_w9, row0, batch, rows, width, tc=256):
    seq = rows * width
    sb0 = row0 // seq
    assert row0 % seq == 0
    return pl.pallas_call(
        functools.partial(_conv_kernel, rows, width),
        grid=(batch, QKV_B // tc),
        in_specs=[
            pl.BlockSpec((seq, tc), lambda b, j: (sb0 + b, j)),
            pl.BlockSpec((9, tc), lambda b, j: (0, j)),
        ],
        out_specs=pl.BlockSpec((seq, tc), lambda b, j: (b, j)),
        out_shape=jax.ShapeDtypeStruct((batch * seq, QKV_B), F32),
        compiler_params=_cparams(("parallel", "parallel")),
        name="conv_silu",
    )(z, conv_w9)


def _outproj_kernel(heads, dvh, of_ref, ob_ref, og_ref, gw_ref, w_ref, x_ref, g_ref, o_ref):
    o = of_ref[...] + ob_ref[...]
    og = og_ref[...]
    gw = gw_ref[...]
    parts = []
    for h in range(heads):
        oh = o[:, h * dvh:(h + 1) * dvh]
        ms = jnp.mean(oh * oh, axis=-1, keepdims=True)
        parts.append(oh * lax.rsqrt(ms + EPS) * gw)
    y = jnp.concatenate(parts, axis=1) * _silu(og)
    o_ref[...] = x_ref[...] + g_ref[0] * _dot(y.astype(BF16), w_ref[...])


def _outproj(o_f, o_b, z, og_col_block, gnorm_w, w_out_bf16, x, mod3, stream, layer, heads, dvh):
    t, d = x.shape
    tm = 256
    st = _Stream(stream.n_prompt_blocks * stream.tm, stream.blocks_per_sample * stream.tm, tm)
    return pl.pallas_call(
        functools.partial(_outproj_kernel, heads, dvh),
        grid=(t // tm,),
        in_specs=[
            pl.BlockSpec((tm, d), lambda i: (i, 0)),
            pl.BlockSpec((tm, d), lambda i: (i, 0)),
            pl.BlockSpec((tm, d), lambda i: (i, og_col_block)),
            pl.BlockSpec((1, dvh), lambda i: (0, 0)),
            pl.BlockSpec((d, d), lambda i: (0, 0)),
            pl.BlockSpec((tm, d), lambda i: (i, 0)),
            pl.BlockSpec((1, 1, d), st.mod_index(layer, 2)),
        ],
        out_specs=pl.BlockSpec((tm, d), lambda i: (i, 0)),
        out_shape=jax.ShapeDtypeStruct((t, d), F32),
        compiler_params=_cparams(("parallel",)),
        name="outproj",
    )(o_f, o_b, z, gnorm_w.reshape(1, dvh), w_out_bf16, x, mod3)


def _top_rows(s, k):
    n, tm = s.shape
    rid = lax.broadcasted_iota(jnp.int32, (n, tm), 0).astype(F32)
    kid = lax.broadcasted_iota(jnp.int32, (k, tm), 0)

    def body(j, carry):
        s, vals, ids = carry
        m = jnp.max(s, axis=0, keepdims=True)
        pos = jnp.min(jnp.where(s == m, rid, float(n)), axis=0, keepdims=True)
        vals = jnp.where(kid == j, m, vals)
        ids = jnp.where(kid == j, pos, ids)
        s = jnp.where(rid == pos, NEG_INF, s)
        return s, vals, ids

    _, vals, ids = lax.fori_loop(0, k, body, (s, jnp.zeros((k, tm), F32), jnp.zeros((k, tm), F32)))
    return vals, ids


_PAIR_COUNTS = [P_TOPK // (a + 1) for a in range(P_TOPK)]
_N_PAIRS = sum(_PAIR_COUNTS)
_N_PAIR_ROWS = -(-_N_PAIRS // 8) * 8


def _route_kernel(q_ref, keys_ref, idx_ref, gate_ref, cand_ref, cid_ref):
    tm = q_ref.shape[0]
    vals, ids = [], []
    for p in range(2):
        s = _dot_nt(keys_ref[0, p], q_ref[:, p * P_HALF:(p + 1) * P_HALF])
        v, i = _top_rows(s, P_TOPK)
        vals.append(v)
        ids.append(i)
    cand_ref[...] = jnp.full(cand_ref.shape, NEG_INF, F32)
    cid_ref[...] = jnp.zeros(cid_ref.shape, F32)
    r0 = 0
    for a, nb in enumerate(_PAIR_COUNTS):
        cand_ref[r0:r0 + nb, :] = vals[0][a:a + 1] + vals[1][0:nb]
        cid_ref[r0:r0 + nb, :] = ids[0][a:a + 1] * float(N_KEYS) + ids[1][0:nb]
        r0 += nb
    cand = cand_ref[...]
    cid = cid_ref[...]
    n = cand.shape[0]
    rid = lax.broadcasted_iota(jnp.int32, (n, tm), 0).astype(F32)
    kid = lax.broadcasted_iota(jnp.int32, (P_TOPK, tm), 0)

    def body(j, carry):
        cand, top, idx = carry
        m = jnp.max(cand, axis=0, keepdims=True)
        pos = jnp.min(jnp.where(cand == m, rid, float(n)), axis=0, keepdims=True)
        sel = rid == pos
        e = jnp.sum(jnp.where(sel, cid, 0.0), axis=0, keepdims=True)
        top = jnp.where(kid == j, m, top)
        idx = jnp.where(kid == j, e, idx)
        return jnp.where(sel, NEG_INF, cand), top, idx

    _, top, idx = lax.fori_loop(
        0, P_TOPK, body, (cand, jnp.zeros((P_TOPK, tm), F32), jnp.zeros((P_TOPK, tm), F32)))
    ex = jnp.exp(top - top[0:1])
    gate_ref[...] = ex / jnp.sum(ex, axis=0, keepdims=True)
    idx_ref[...] = idx.astype(jnp.int32)


def _route(q, keys):
    t = q.shape[0]
    tm = 256
    return pl.pallas_call(
        _route_kernel,
        grid=(t // tm, P_HEADS),
        in_specs=[
            pl.BlockSpec((tm, 2 * P_HALF), lambda i, h: (i, h)),
            pl.BlockSpec((1, 2, N_KEYS, P_HALF), lambda i, h: (h, 0, 0, 0)),
        ],
        out_specs=[
            pl.BlockSpec((P_TOPK, tm), lambda i, h: (h, i)),
            pl.BlockSpec((P_TOPK, tm), lambda i, h: (h, i)),
        ],
        out_shape=[
            jax.ShapeDtypeStruct((N_SEL, t), jnp.int32),
            jax.ShapeDtypeStruct((N_SEL, t), F32),
        ],
        scratch_shapes=[pltpu.VMEM((_N_PAIR_ROWS, tm), F32), pltpu.VMEM((_N_PAIR_ROWS, tm), F32)],
        compiler_params=_cparams(("parallel", "parallel")),
        name="peer_route",
    )(q, keys)


PEER_TB = 64
SUBLANES = 8
D_TILES = D_MODEL // LANE


def _sublane_sums(p):
    s = lax.broadcasted_iota(jnp.int32, p.shape[2:], 0)
    sel = s < 4
    a, b = p[:, 0:4], p[:, 4:8]
    f = jnp.where(sel, a, b) + pltpu.roll(jnp.where(sel, b, a), 4, 2)
    sel = (s & 2) == 0
    a, b = f[:, 0:2], f[:, 2:4]
    g = jnp.where(sel, a + pltpu.roll(a, 6, 2), b + pltpu.roll(b, 2, 2))
    sel = (s & 1) == 0
    a, b = g[:, 0], g[:, 1]
    return jnp.where(sel, a + pltpu.roll(a, 7, 1), b + pltpu.roll(b, 1, 1))


def _peer_kernel(idx_ref, h_ref, gate_ref, x_ref, g2_ref, uv_hbm, o_ref, buf, cb_ref, sem):
    def start_token(t, slot):
        for e in range(N_SEL):
            pltpu.make_async_copy(uv_hbm.at[idx_ref[t, e]], buf.at[slot, e], sem.at[slot]).start(priority=e % 2)

    def wait_token(slot):
        pltpu.make_async_copy(uv_hbm.at[pl.ds(0, N_SEL)], buf.at[slot], sem.at[slot]).wait()

    eye = lax.broadcasted_iota(jnp.int32, (N_SEL, N_SEL), 0) == lax.broadcasted_iota(jnp.int32, (N_SEL, N_SEL), 1)
    g2 = g2_ref[0]

    def compute(t, slot):
        prod = buf[slot, :, 0:SUBLANES, :] * h_ref[t]
        rows = _sublane_sums(prod.reshape(N_SEL // 8, 8, SUBLANES, LANE))
        s = jnp.sum(rows.reshape(N_SEL, LANE), axis=1, keepdims=True)
        gcol = jnp.sum(jnp.where(eye, gate_ref[pl.ds(t, 1), :], 0.0), axis=1, keepdims=True)
        cb_ref[...] = jnp.broadcast_to(gcol * _gelu(s), (N_SEL, LANE))
        accs = [jnp.zeros((SUBLANES, LANE), F32) for _ in range(4)]
        for e in range(N_SEL):
            accs[e % 4] = accs[e % 4] + cb_ref[pl.ds(e, 1), :] * buf[slot, e, SUBLANES:2 * SUBLANES, :]
        o_ref[t] = x_ref[t] + g2 * ((accs[0] + accs[1]) + (accs[2] + accs[3]))

    start_token(0, 0)

    def body(i, _):
        t = 2 * i
        start_token(t + 1, 1)
        wait_token(0)
        compute(t, 0)

        @pl.when(t + 2 < PEER_TB)
        def _():
            start_token(t + 2, 0)

        wait_token(1)
        compute(t + 1, 1)
        return 0

    lax.fori_loop(0, PEER_TB // 2, body, 0)


def _peer(idx_t, gate_t, h, x, mod3, stream, layer, uv3):
    t, d = x.shape
    tb = PEER_TB
    st = _Stream(stream.n_prompt_blocks * stream.tm, stream.blocks_per_sample * stream.tm, tb)
    tok_spec = pl.BlockSpec((tb, D_TILES, LANE), lambda i: (i, 0, 0))
    out = pl.pallas_call(
        _peer_kernel,
        grid=(t // tb,),
        in_specs=[
            pl.BlockSpec((tb, N_SEL), lambda i: (i, 0), memory_space=pltpu.SMEM),
            tok_spec,
            pl.BlockSpec((tb, N_SEL), lambda i: (i, 0)),
            tok_spec,
            pl.BlockSpec((1, D_TILES, LANE), st.mod_index(layer, 5)),
            pl.BlockSpec(memory_space=pl.ANY),
        ],
        out_specs=tok_spec,
        out_shape=jax.ShapeDtypeStruct((t, D_TILES, LANE), F32),
        scratch_shapes=[
            pltpu.VMEM((2, N_SEL, 2 * SUBLANES, LANE), F32),
            pltpu.VMEM((N_SEL, LANE), F32),
            pltpu.SemaphoreType.DMA((2,)),
        ],
        compiler_params=_cparams(("arbitrary",)),
        name="peer_experts",
    )(idx_t, h.reshape(t, D_TILES, LANE), gate_t, x.reshape(t, D_TILES, LANE),
      mod3.reshape(-1, D_TILES, LANE), uv3)
    return out.reshape(t, d)


def _rms_kernel(x_ref, w_ref, o_ref):
    x = x_ref[...]
    ms = jnp.mean(x * x, axis=-1, keepdims=True)
    o_ref[...] = x * lax.rsqrt(ms + EPS) * w_ref[...]


def _rms_norm(x, w):
    t, d = x.shape
    tm = 512
    return pl.pallas_call(
        _rms_kernel,
        grid=(t // tm,),
        in_specs=[pl.BlockSpec((tm, d), lambda i: (i, 0)), pl.BlockSpec((1, d), lambda i: (0, 0))],
        out_specs=pl.BlockSpec((tm, d), lambda i: (i, 0)),
        out_shape=jax.ShapeDtypeStruct((t, d), F32),
        compiler_params=_cparams(("parallel",)),
        name="final_norm",
    )(x, w.reshape(1, d))


def kernel(x_prompt, x_sample, state_hgrn, state_gla, c, c_ctx, ada_w, ada_b, norm1_w, norm2_w, final_norm_w, hgrn_w_in, hgrn_lb, hgrn_gnorm_w, hgrn_w_out, gla_w_in, gla_conv_w, gla_w_gate, gla_b_gate, gla_gnorm_w, gla_w_out, peer_w_q, peer_keys, peer_u, peer_v):
    bp, lp, d = x_prompt.shape
    bs, ls, _ = x_sample.shape
    t_p, t_s = bp * lp, bs * ls
    rows_s = ls // GRID_W
    stream = _Stream(t_p, ls, ROW_TILE)

    x = jnp.concatenate([x_prompt.reshape(t_p, d), x_sample.reshape(t_s, d)], axis=0)

    cvec = jnp.concatenate([c_ctx[None, :], c, jnp.zeros((MOD_ROWS - 1 - bs, d), F32)], axis=0)
    mod3 = _adaln(cvec, ada_w, ada_b).reshape(DEPTH * MOD_ROWS * 6, 1, d)

    p = jax.nn.softmax(hgrn_lb.astype(F32), axis=0)
    lower_bounds = jnp.cumsum(p, axis=0) - p[0:1]

    gla_n = gla_w_in.shape[2]
    gla_np = -(-gla_n // LANE) * LANE
    low_col_block = (QKV_B + D_MODEL) // LANE
    n_b = gla_w_in.shape[0]
    wg_pad = jnp.zeros((n_b, 2, LANE, H_B * DK_B), F32)
    wg_pad = wg_pad.at[:, 0, 0:GLA_RANK].set(gla_w_gate[:, 0]).at[:, 1, GLA_RANK:2 * GLA_RANK].set(gla_w_gate[:, 1])

    new_hgrn, new_gla = [], []
    for l in range(DEPTH):
        j = l // 2
        if l % 2 == 0:
            z = _norm_mod_matmul(x, norm1_w[l], mod3, stream, l, 0, 1, hgrn_w_in[j].astype(BF16), 512)
            lb = lower_bounds[j]
            of_p, ob_p, sf, sb = _hgrn_scan(z, lb, None, 0, bp, lp)
            init = (jnp.swapaxes(state_hgrn[:, j, 0], -1, -2), jnp.swapaxes(state_hgrn[:, j, 1], -1, -2))
            of_s, ob_s, _, _ = _hgrn_scan(z, lb, init, t_p, bs, ls)
            new_hgrn.append(jnp.stack([jnp.swapaxes(sf, -1, -2), jnp.swapaxes(sb, -1, -2)], axis=1))
            heads, dvh, og_blk = H_A, DV_A, 4
            gnw, w_out = hgrn_gnorm_w[j], hgrn_w_out[j]
        else:
            w_in = jnp.pad(gla_w_in[j], ((0, 0), (0, gla_np - gla_n))).astype(BF16)
            z = _norm_mod_matmul(x, norm1_w[l], mod3, stream, l, 0, 1, w_in, gla_np // 5 if gla_np % 5 == 0 else LANE)
            cw = gla_conv_w[j].reshape(9, QKV_B)
            y_p = _conv_silu(z, cw, 0, bp, 1, lp)
            y_s = _conv_silu(z, cw, t_p, bs, rows_s, GRID_W)
            wg, bg = wg_pad[j], gla_b_gate[j]
            of_p, ob_p, sf, sb = _gla_scan(y_p, z, wg, bg, None, 0, bp, lp, low_col_block)
            init = (jnp.swapaxes(state_gla[:, j, 0], -1, -2), jnp.swapaxes(state_gla[:, j, 1], -1, -2))
            of_s, ob_s, _, _ = _gla_scan(y_s, z, wg, bg, init, t_p, bs, ls, low_col_block)
            new_gla.append(jnp.stack([jnp.swapaxes(sf, -1, -2), jnp.swapaxes(sb, -1, -2)], axis=1))
            heads, dvh, og_blk = H_B, DV_B, 2
            gnw, w_out = gla_gnorm_w[j], gla_w_out[j]
        o_f = jnp.concatenate([of_p, of_s], axis=0)
        o_b = jnp.concatenate([ob_p, ob_s], axis=0)
        x = _outproj(o_f, o_b, z, og_blk, gnw, w_out.astype(BF16), x, mod3, stream, l, heads, dvh)

        q, h = _norm_mod_matmul(x, norm2_w[l], mod3, stream, l, 3, 4, peer_w_q[l].astype(BF16), 512, emit_h=True)
        idx, gates = _route(q, peer_keys[l])
        n_exp = peer_u.shape[1]
        uv3 = jnp.concatenate([peer_u[l].reshape(n_exp, D_TILES, LANE), peer_v[l].reshape(n_exp, D_TILES, LANE)], axis=1)
        x = _peer(idx.T, gates.T, h, x, mod3, stream, l, uv3)

    y = _rms_norm(x, final_norm_w)
    y_prompt = y[:t_p].reshape(bp, lp, d)
    y_sample = y[t_p:].reshape(bs, ls, d)
    return (y_prompt, y_sample, jnp.stack(new_hgrn, axis=1), jnp.stack(new_gla, axis=1))
```

```python
import functools

import jax
import jax.numpy as jnp
from jax import lax
from jax.experimental import pallas as pl
from jax.experimental.pallas import tpu as pltpu

F32 = jnp.float32
BF16 = jnp.bfloat16

D_MODEL = 1024
DEPTH = 4
GRID_W = 64
EPS = 1e-6
F_MIN = 1e-20
H_A, DK_A, DV_A = 8, 128, 128
H_B, DK_B, DV_B = 4, 128, 256
QKV_B = 2048
GLA_RANK = 16
GATE_NORM = 16.0
P_HEADS, N_KEYS, P_TOPK, P_HALF = 8, 128, 16, 128
N_SEL = P_HEADS * P_TOPK

V7X_VMEM_LIMIT = 48 * 1024 * 1024
LANE = 128
ROW_TILE = 512
CHUNK = 64
SUB = 16
MOD_ROWS = 16
NEG_INF = float("-inf")


def _cparams(sem):
    return pltpu.CompilerParams(dimension_semantics=sem, vmem_limit_bytes=V7X_VMEM_LIMIT)


def _silu(x):
    return x * jax.nn.sigmoid(x)


def _gelu(x):
    return 0.5 * x * (1.0 + lax.erf(x * (2.0 ** -0.5)))


def _dot(a, b):
    return jnp.dot(a, b, preferred_element_type=F32)


def _dot_nt(a, b):
    return lax.dot_general(a, b, (((1,), (1,)), ((), ())), preferred_element_type=F32)


def _dot_tn(a, b):
    return lax.dot_general(a, b, (((0,), (0,)), ((), ())), preferred_element_type=F32)


def _dot_01(m01, x):
    x1 = x.astype(BF16)
    r1 = x - x1.astype(F32)
    x2 = r1.astype(BF16)
    x3 = (r1 - x2.astype(F32)).astype(BF16)
    return _dot(m01, x1) + _dot(m01, x2) + _dot(m01, x3)


def _adaln_kernel(c_ref, w_ref, b_ref, o_ref):
    o_ref[0] = _dot(_silu(c_ref[...]), w_ref[0]) + b_ref[0]


def _adaln(cvec, ada_w, ada_b):
    tn = 1536
    n = 6 * D_MODEL
    return pl.pallas_call(
        _adaln_kernel,
        grid=(DEPTH, n // tn),
        in_specs=[
            pl.BlockSpec((MOD_ROWS, D_MODEL), lambda l, j: (0, 0)),
            pl.BlockSpec((1, D_MODEL, tn), lambda l, j: (l, 0, j)),
            pl.BlockSpec((1, 1, tn), lambda l, j: (l, 0, j)),
        ],
        out_specs=pl.BlockSpec((1, MOD_ROWS, tn), lambda l, j: (l, 0, j)),
        out_shape=jax.ShapeDtypeStruct((DEPTH, MOD_ROWS, n), F32),
        compiler_params=_cparams(("parallel", "parallel")),
        name="adaln",
    )(cvec, ada_w, ada_b.reshape(DEPTH, 1, n))


class _Stream:
    def __init__(self, t_prompt, seq_sample, tm):
        assert t_prompt % tm == 0 and seq_sample % tm == 0
        self.tm = tm
        self.n_prompt_blocks = t_prompt // tm
        self.blocks_per_sample = seq_sample // tm

    def mod_index(self, layer, k):
        npb, bps = self.n_prompt_blocks, self.blocks_per_sample

        def f(i, *_):
            row = jnp.where(i < npb, 0, 1 + (i - npb) // bps)
            return ((layer * MOD_ROWS + row) * 6 + k, 0, 0)

        return f


def _norm_mod(x, nw, shift, scale):
    ms = jnp.mean(x * x, axis=-1, keepdims=True)
    return (x * lax.rsqrt(ms + EPS) * nw) * (1.0 + scale) + shift


def _nmm_kernel(emit_h, x_ref, nw_ref, sh_ref, sc_ref, w_ref, *rest):
    if emit_h:
        o_ref, h_ref, hs_ref = rest
    else:
        o_ref, hs_ref = rest

    @pl.when(pl.program_id(1) == 0)
    def _():
        h = _norm_mod(x_ref[...], nw_ref[...], sh_ref[0], sc_ref[0])
        hs_ref[...] = h.astype(BF16)
        if emit_h:
            h_ref[...] = h

    o_ref[...] = _dot(hs_ref[...], w_ref[...])


def _norm_mod_matmul(x, nw, mod3, stream, layer, k_shift, k_scale, w_bf16, tn, emit_h=False):
    t, d = x.shape
    n = w_bf16.shape[1]
    tm = stream.tm
    assert n % tn == 0
    out_shape = [jax.ShapeDtypeStruct((t, n), F32)]
    out_specs = [pl.BlockSpec((tm, tn), lambda i, j: (i, j))]
    if emit_h:
        out_shape.append(jax.ShapeDtypeStruct((t, d), F32))
        out_specs.append(pl.BlockSpec((tm, d), lambda i, j: (i, 0)))
    res = pl.pallas_call(
        functools.partial(_nmm_kernel, emit_h),
        grid=(t // tm, n // tn),
        in_specs=[
            pl.BlockSpec((tm, d), lambda i, j: (i, 0)),
            pl.BlockSpec((1, d), lambda i, j: (0, 0)),
            pl.BlockSpec((1, 1, d), stream.mod_index(layer, k_shift)),
            pl.BlockSpec((1, 1, d), stream.mod_index(layer, k_scale)),
            pl.BlockSpec((d, tn), lambda i, j: (0, j)),
        ],
        out_specs=out_specs,
        out_shape=out_shape,
        scratch_shapes=[pltpu.VMEM((tm, d), BF16)],
        compiler_params=_cparams(("parallel", "arbitrary")),
        name="norm_mod_matmul",
    )(x, nw.reshape(1, d), mod3, mod3, w_bf16)
    return res if emit_h else res[0]


def _scan_direction(q, k, v, g, st_ref, sidx, rev):
    c = CHUNK
    row = lax.broadcasted_iota(jnp.int32, (c, c), 0)
    col = lax.broadcasted_iota(jnp.int32, (c, c), 1)
    cmat = jnp.where((col >= row) if rev else (col <= row), 1.0, 0.0).astype(BF16)
    cs = _dot_01(cmat, g)
    st = st_ref[sidx]
    o = _dot_nt(q * jnp.exp(cs), st)
    edge = cs[0:1] if rev else cs[c - 1:c]
    st_ref[sidx] = st * jnp.exp(edge) + _dot_tn(v, k * jnp.exp(edge - cs))

    srow = lax.broadcasted_iota(jnp.int32, (SUB, c), 0)
    scol = lax.broadcasted_iota(jnp.int32, (SUB, c), 1)
    att_rows = []
    for blk in range(c // SUB):
        r0 = blk * SUB
        q_i, k_i, cs_i = q[r0:r0 + SUB], k[r0:r0 + SUB], cs[r0:r0 + SUB]
        has_off = (blk < c // SUB - 1) if rev else (blk > 0)
        if has_off:
            ref = cs[r0 + SUB:r0 + SUB + 1] if rev else cs[r0 - 1:r0]
            w_i = cs_i - ref
            kt = k * jnp.exp(jnp.minimum(ref - cs, 0.0))
            off = _dot_nt(q_i * jnp.exp(w_i), kt)
            off_mask = (scol >= r0 + SUB) if rev else (scol < r0)
            a_i = jnp.where(off_mask, off, 0.0)
        else:
            w_i = cs_i
            a_i = jnp.zeros((SUB, c), F32)
        for s in range(SUB):
            e = jnp.exp(jnp.minimum(w_i - w_i[s:s + 1], 0.0))
            colv = jnp.sum(q_i * e * k_i[s:s + 1], axis=1, keepdims=True)
            keep = (srow <= s) if rev else (srow >= s)
            a_i = jnp.where((scol == r0 + s) & keep, colv, a_i)
        att_rows.append(a_i)
    att = jnp.concatenate(att_rows, axis=0)
    return o + _dot(att, v)


def _log_sigmoid(x):
    return jnp.minimum(x, 0.0) - jnp.log(1.0 + jnp.exp(-jnp.abs(x)))


def _scan_kernel(mode, zero_init, *refs):
    if mode == "hgrn":
        (qf, vf, ff, qb, vb, fb, lbf, lbb), rest = refs[:8], refs[8:]
    else:
        (qf, kf, vf, lowf, qb, kb, vb, lowb, wgf, wgb, bgf, bgb), rest = refs[:12], refs[12:]
    if not zero_init:
        (s0f, s0b), rest = rest[:2], rest[2:]
    of_ref, ob_ref, sf_ref, sb_ref, st_ref = rest
    ci = pl.program_id(2)

    @pl.when(ci == 0)
    def _():
        if zero_init:
            st_ref[...] = jnp.zeros_like(st_ref)
        else:
            st_ref[0] = s0f[0, 0]
            st_ref[1] = s0b[0, 0]

    def prep(q_ref, k_ref, f_ref, lb_ref, low_ref, wg_ref, bg_ref):
        if mode == "hgrn":
            lb = lb_ref[...]
            fp = f_ref[...]
            f = lb + (1.0 - lb) * jax.nn.sigmoid(fp)
            g = jnp.log(jnp.maximum(f, F_MIN))
            k = (1.0 - lb) * jax.nn.sigmoid(-fp)
            q = _silu(q_ref[...])
        else:
            gk = _dot(low_ref[...], wg_ref[...]) + bg_ref[...]
            g = _log_sigmoid(gk) * (1.0 / GATE_NORM)
            k = k_ref[...]
            q = q_ref[...] * (DK_B ** -0.5)
        return q, k, g

    if mode == "hgrn":
        q, k, g = prep(qf, None, ff, lbf, None, None, None)
        of_ref[...] = _scan_direction(q, k, vf[...], g, st_ref, 0, False)
        q, k, g = prep(qb, None, fb, lbb, None, None, None)
        ob_ref[...] = _scan_direction(q, k, vb[...], g, st_ref, 1, True)
    else:
        q, k, g = prep(qf, kf, None, None, lowf, wgf, bgf)
        of_ref[...] = _scan_direction(q, k, vf[...], g, st_ref, 0, False)
        q, k, g = prep(qb, kb, None, None, lowb, wgb, bgb)
        ob_ref[...] = _scan_direction(q, k, vb[...], g, st_ref, 1, True)

    @pl.when(ci == pl.num_programs(2) - 1)
    def _():
        sf_ref[0, 0] = st_ref[0]
        sb_ref[0, 0] = st_ref[1]


def _scan(mode, arrays, consts, init_states, batch, seq, heads, dk, dv):
    n = seq // CHUNK
    zero_init = init_states is None

    def chunk_of(c, dirn):
        return c if dirn == 0 else n - 1 - c

    in_specs, operands = [], []
    for dirn in (0, 1):
        for arr, width, row0, cols, per_head in arrays:
            assert row0 % CHUNK == 0

            def imap(b, h, c, dirn=dirn, rb0=row0 // CHUNK, col0=cols[dirn], per_head=per_head):
                return (rb0 + b * n + chunk_of(c, dirn), col0 + (h if per_head else 0))

            in_specs.append(pl.BlockSpec((CHUNK, width), imap))
            operands.append(arr)
    for arr, shape in consts:
        in_specs.append(pl.BlockSpec(shape, lambda b, h, c: (0, h)))
        operands.append(arr)
    if not zero_init:
        for s0 in init_states:
            in_specs.append(pl.BlockSpec((1, 1, dv, dk), lambda b, h, c: (b, h, 0, 0)))
            operands.append(s0)
    o_specs = [pl.BlockSpec((CHUNK, dv), lambda b, h, c, dirn=dirn: (b * n + chunk_of(c, dirn), h)) for dirn in (0, 1)]
    s_spec = pl.BlockSpec((1, 1, dv, dk), lambda b, h, c: (b, h, 0, 0))
    return pl.pallas_call(
        functools.partial(_scan_kernel, mode, zero_init),
        grid=(batch, heads, n),
        in_specs=in_specs,
        out_specs=o_specs + [s_spec, s_spec],
        out_shape=[
            jax.ShapeDtypeStruct((batch * seq, heads * dv), F32),
            jax.ShapeDtypeStruct((batch * seq, heads * dv), F32),
            jax.ShapeDtypeStruct((batch, heads, dv, dk), F32),
            jax.ShapeDtypeStruct((batch, heads, dv, dk), F32),
        ],
        scratch_shapes=[pltpu.VMEM((2, dv, dk), F32)],
        compiler_params=_cparams(("parallel", "parallel", "arbitrary")),
        name="scan_" + mode,
    )(*operands)


def _hgrn_scan(z, lb, init_states, row0, batch, seq):
    nh = H_A
    arrays = [
        (z, DK_A, row0, (0, 0), True),
        (z, DV_A, row0, (nh, nh), True),
        (z, DK_A, row0, (2 * nh, 3 * nh), True),
    ]
    consts = [(lb[0:1], (1, DK_A)), (lb[1:2], (1, DK_A))]
    return _scan("hgrn", arrays, consts, init_states, batch, seq, H_A, DK_A, DV_A)


def _gla_scan(y, z, wg, bg, init_states, row0, batch, seq, low_col_block):
    v0 = 2 * H_B * DK_B // DV_B
    arrays = [
        (y, DK_B, 0, (0, 0), True),
        (y, DK_B, 0, (H_B, H_B), True),
        (y, DV_B, 0, (v0, v0), True),
        (z, LANE, row0, (low_col_block, low_col_block), False),
    ]
    consts = [(wg[0], (LANE, DK_B)), (wg[1], (LANE, DK_B)), (bg[0:1], (1, DK_B)), (bg[1:2], (1, DK_B))]
    return _scan("gla", arrays, consts, init_states, batch, seq, H_B, DK_B, DV_B)


def _conv_kernel(rows, width, x_ref, w_ref, o_ref):
    seq = rows * width
    x = x_ref[...]
    assert width & (width - 1) == 0
    l = lax.broadcasted_iota(jnp.int32, (seq, 1), 0)
    r, c = l >> (width.bit_length() - 1), l & (width - 1)
    acc = jnp.zeros_like(x)
    for di in range(3):
        if rows == 1 and di != 1:
            continue
        for dj in range(3):
            k = (di - 1) * width + (dj - 1)
            xs = x if k == 0 else pltpu.roll(x, (-k) % seq, 0)
            ok = (r + (di - 1) >= 0) & (r + (di - 1) < rows) & (c + (dj - 1) >= 0) & (c + (dj - 1) < width)
            acc = acc + jnp.where(ok, xs, 0.0) * w_ref[3 * di + dj:3 * di + dj + 1, :]
    o_ref[...] = _silu(acc)


def _conv_silu(z, conv_w9, row0, batch, rows, width, tc=256):
    seq = rows * width
    sb0 = row0 // seq
    assert row0 % seq == 0
    return pl.pallas_call(
        functools.partial(_conv_kernel, rows, width),
        grid=(batch, QKV_B // tc),
        in_specs=[
            pl.BlockSpec((seq, tc), lambda b, j: (sb0 + b, j)),
            pl.BlockSpec((9, tc), lambda b, j: (0, j)),
        ],
        out_specs=pl.BlockSpec((seq, tc), lambda b, j: (b, j)),
        out_shape=jax.ShapeDtypeStruct((batch * seq, QKV_B), F32),
        compiler_params=_cparams(("parallel", "parallel")),
        name="conv_silu",
    )(z, conv_w9)


def _outproj_kernel(heads, dvh, of_ref, ob_ref, og_ref, gw_ref, w_ref, x_ref, g_ref, o_ref):
    o = of_ref[...] + ob_ref[...]
    og = og_ref[...]
    gw = gw_ref[...]
    parts = []
    for h in range(heads):
        oh = o[:, h * dvh:(h + 1) * dvh]
        ms = jnp.mean(oh * oh, axis=-1, keepdims=True)
        parts.append(oh * lax.rsqrt(ms + EPS) * gw)
    y = jnp.concatenate(parts, axis=1) * _silu(og)
    o_ref[...] = x_ref[...] + g_ref[0] * _dot(y.astype(BF16), w_ref[...])


def _outproj(o_f, o_b, z, og_col_block, gnorm_w, w_out_bf16, x, mod3, stream, layer, heads, dvh):
    t, d = x.shape
    tm = 256
    st = _Stream(stream.n_prompt_blocks * stream.tm, stream.blocks_per_sample * stream.tm, tm)
    return pl.pallas_call(
        functools.partial(_outproj_kernel, heads, dvh),
        grid=(t // tm,),
        in_specs=[
            pl.BlockSpec((tm, d), lambda i: (i, 0)),
            pl.BlockSpec((tm, d), lambda i: (i, 0)),
            pl.BlockSpec((tm, d), lambda i: (i, og_col_block)),
            pl.BlockSpec((1, dvh), lambda i: (0, 0)),
            pl.BlockSpec((d, d), lambda i: (0, 0)),
            pl.BlockSpec((tm, d), lambda i: (i, 0)),
            pl.BlockSpec((1, 1, d), st.mod_index(layer, 2)),
        ],
        out_specs=pl.BlockSpec((tm, d), lambda i: (i, 0)),
        out_shape=jax.ShapeDtypeStruct((t, d), F32),
        compiler_params=_cparams(("parallel",)),
        name="outproj",
    )(o_f, o_b, z, gnorm_w.reshape(1, dvh), w_out_bf16, x, mod3)


def _top_rows(s, k):
    n, tm = s.shape
    rid = lax.broadcasted_iota(jnp.int32, (n, tm), 0).astype(F32)
    kid = lax.broadcasted_iota(jnp.int32, (k, tm), 0)

    def body(j, carry):
        s, vals, ids = carry
        m = jnp.max(s, axis=0, keepdims=True)
        pos = jnp.min(jnp.where(s == m, rid, float(n)), axis=0, keepdims=True)
        vals = jnp.where(kid == j, m, vals)
        ids = jnp.where(kid == j, pos, ids)
        s = jnp.where(rid == pos, NEG_INF, s)
        return s, vals, ids

    _, vals, ids = lax.fori_loop(0, k, body, (s, jnp.zeros((k, tm), F32), jnp.zeros((k, tm), F32)))
    return vals, ids


_PAIR_COUNTS = [P_TOPK // (a + 1) for a in range(P_TOPK)]
_N_PAIRS = sum(_PAIR_COUNTS)
_N_PAIR_ROWS = -(-_N_PAIRS // 8) * 8


def _route_kernel(q_ref, keys_ref, idx_ref, gate_ref, cand_ref, cid_ref):
    tm = q_ref.shape[0]
    vals, ids = [], []
    for p in range(2):
        s = _dot_nt(keys_ref[0, p], q_ref[:, p * P_HALF:(p + 1) * P_HALF])
        v, i = _top_rows(s, P_TOPK)
        vals.append(v)
        ids.append(i)
    cand_ref[...] = jnp.full(cand_ref.shape, NEG_INF, F32)
    cid_ref[...] = jnp.zeros(cid_ref.shape, F32)
    r0 = 0
    for a, nb in enumerate(_PAIR_COUNTS):
        cand_ref[r0:r0 + nb, :] = vals[0][a:a + 1] + vals[1][0:nb]
        cid_ref[r0:r0 + nb, :] = ids[0][a:a + 1] * float(N_KEYS) + ids[1][0:nb]
        r0 += nb
    cand = cand_ref[...]
    cid = cid_ref[...]
    n = cand.shape[0]
    rid = lax.broadcasted_iota(jnp.int32, (n, tm), 0).astype(F32)
    kid = lax.broadcasted_iota(jnp.int32, (P_TOPK, tm), 0)

    def body(j, carry):
        cand, top, idx = carry
        m = jnp.max(cand, axis=0, keepdims=True)
        pos = jnp.min(jnp.where(cand == m, rid, float(n)), axis=0, keepdims=True)
        sel = rid == pos
        e = jnp.sum(jnp.where(sel, cid, 0.0), axis=0, keepdims=True)
        top = jnp.where(kid == j, m, top)
        idx = jnp.where(kid == j, e, idx)
        return jnp.where(sel, NEG_INF, cand), top, idx

    _, top, idx = lax.fori_loop(
        0, P_TOPK, body, (cand, jnp.zeros((P_TOPK, tm), F32), jnp.zeros((P_TOPK, tm), F32)))
    ex = jnp.exp(top - top[0:1])
    gate_ref[...] = ex / jnp.sum(ex, axis=0, keepdims=True)
    idx_ref[...] = idx.astype(jnp.int32)


def _route(q, keys):
    t = q.shape[0]
    tm = 256
    return pl.pallas_call(
        _route_kernel,
        grid=(t // tm, P_HEADS),
        in_specs=[
            pl.BlockSpec((tm, 2 * P_HALF), lambda i, h: (i, h)),
            pl.BlockSpec((1, 2, N_KEYS, P_HALF), lambda i, h: (h, 0, 0, 0)),
        ],
        out_specs=[
            pl.BlockSpec((P_TOPK, tm), lambda i, h: (h, i)),
            pl.BlockSpec((P_TOPK, tm), lambda i, h: (h, i)),
        ],
        out_shape=[
            jax.ShapeDtypeStruct((N_SEL, t), jnp.int32),
            jax.ShapeDtypeStruct((N_SEL, t), F32),
        ],
        scratch_shapes=[pltpu.VMEM((_N_PAIR_ROWS, tm), F32), pltpu.VMEM((_N_PAIR_ROWS, tm), F32)],
        compiler_params=_cparams(("parallel", "parallel")),
        name="peer_route",
    )(q, keys)


PEER_TB = 128
PEER_NBUF = 4
SUBLANES = 8
D_TILES = D_MODEL // LANE


def _sublane_sums(p):
    s = lax.broadcasted_iota(jnp.int32, p.shape[2:], 0)
    sel = s < 4
    a, b = p[:, 0:4], p[:, 4:8]
    f = jnp.where(sel, a, b) + pltpu.roll(jnp.where(sel, b, a), 4, 2)
    sel = (s & 2) == 0
    a, b = f[:, 0:2], f[:, 2:4]
    g = jnp.where(sel, a + pltpu.roll(a, 6, 2), b + pltpu.roll(b, 2, 2))
    sel = (s & 1) == 0
    a, b = g[:, 0], g[:, 1]
    return jnp.where(sel, a + pltpu.roll(a, 7, 1), b + pltpu.roll(b, 1, 1))


def _peer_kernel(idx_ref, h_ref, gate_ref, x_ref, g2_ref, uv_hbm, o_ref, *scratch):
    bufs, cb_ref, sem = scratch[:PEER_NBUF], scratch[PEER_NBUF], scratch[PEER_NBUF + 1]

    def start_token(t, slot):
        for e in range(N_SEL):
            pltpu.make_async_copy(uv_hbm.at[idx_ref[t, e]], bufs[slot].at[e], sem.at[slot]).start(priority=e % 2)

    def wait_token(slot):
        pltpu.make_async_copy(uv_hbm.at[pl.ds(0, N_SEL)], bufs[slot], sem.at[slot]).wait()

    eye = lax.broadcasted_iota(jnp.int32, (N_SEL, N_SEL), 0) == lax.broadcasted_iota(jnp.int32, (N_SEL, N_SEL), 1)
    g2 = g2_ref[0]

    def compute(t, slot):
        buf = bufs[slot]
        prod = buf[:, 0:SUBLANES, :] * h_ref[t]
        rows = _sublane_sums(prod.reshape(N_SEL // 8, 8, SUBLANES, LANE))
        s = jnp.sum(rows.reshape(N_SEL, LANE), axis=1, keepdims=True)
        gcol = jnp.sum(jnp.where(eye, gate_ref[pl.ds(t, 1), :], 0.0), axis=1, keepdims=True)
        cb_ref[...] = jnp.broadcast_to(gcol * _gelu(s), (N_SEL, LANE))
        accs = [jnp.zeros((SUBLANES, LANE), F32) for _ in range(4)]
        for e in range(N_SEL):
            accs[e % 4] = accs[e % 4] + cb_ref[pl.ds(e, 1), :] * buf[e, SUBLANES:2 * SUBLANES, :]
        o_ref[t] = x_ref[t] + g2 * ((accs[0] + accs[1]) + (accs[2] + accs[3]))

    ahead = PEER_NBUF - 1
    for t in range(ahead):
        start_token(t, t)

    def group(t0, last):
        for k in range(PEER_NBUF):
            wait_token(k)
            if not last:
                start_token(t0 + k + ahead, (k + ahead) % PEER_NBUF)
            elif k + ahead < PEER_NBUF:
                start_token(PEER_TB - PEER_NBUF + k + ahead, (k + ahead) % PEER_NBUF)
            compute(t0 + k, k)

    def body(i, _):
        group(PEER_NBUF * i, False)
        return 0

    lax.fori_loop(0, PEER_TB // PEER_NBUF - 1, body, 0)
    group(PEER_TB - PEER_NBUF, True)


def _peer(idx_t, gate_t, h, x, mod3, stream, layer, uv3):
    t, d = x.shape
    tb = PEER_TB
    st = _Stream(stream.n_prompt_blocks * stream.tm, stream.blocks_per_sample * stream.tm, tb)
    tok_spec = pl.BlockSpec((tb, D_TILES, LANE), lambda i: (i, 0, 0))
    out = pl.pallas_call(
        _peer_kernel,
        grid=(t // tb,),
        in_specs=[
            pl.BlockSpec((tb, N_SEL), lambda i: (i, 0), memory_space=pltpu.SMEM),
            tok_spec,
            pl.BlockSpec((tb, N_SEL), lambda i: (i, 0)),
            tok_spec,
            pl.BlockSpec((1, D_TILES, LANE), st.mod_index(layer, 5)),
            pl.BlockSpec(memory_space=pl.ANY),
        ],
        out_specs=tok_spec,
        out_shape=jax.ShapeDtypeStruct((t, D_TILES, LANE), F32),
        scratch_shapes=[pltpu.VMEM((N_SEL, 2 * SUBLANES, LANE), F32) for _ in range(PEER_NBUF)] + [
            pltpu.VMEM((N_SEL, LANE), F32),
            pltpu.SemaphoreType.DMA((PEER_NBUF,)),
        ],
        compiler_params=_cparams(("arbitrary",)),
        name="peer_experts",
    )(idx_t, h.reshape(t, D_TILES, LANE), gate_t, x.reshape(t, D_TILES, LANE),
      mod3.reshape(-1, D_TILES, LANE), uv3)
    return out.reshape(t, d)


def _rms_kernel(x_ref, w_ref, o_ref):
    x = x_ref[...]
    ms = jnp.mean(x * x, axis=-1, keepdims=True)
    o_ref[...] = x * lax.rsqrt(ms + EPS) * w_ref[...]


def _rms_norm(x, w):
    t, d = x.shape
    tm = 512
    return pl.pallas_call(
        _rms_kernel,
        grid=(t // tm,),
        in_specs=[pl.BlockSpec((tm, d), lambda i: (i, 0)), pl.BlockSpec((1, d), lambda i: (0, 0))],
        out_specs=pl.BlockSpec((tm, d), lambda i: (i, 0)),
        out_shape=jax.ShapeDtypeStruct((t, d), F32),
        compiler_params=_cparams(("parallel",)),
        name="final_norm",
    )(x, w.reshape(1, d))


def kernel(x_prompt, x_sample, state_hgrn, state_gla, c, c_ctx, ada_w, ada_b, norm1_w, norm2_w, final_norm_w, hgrn_w_in, hgrn_lb, hgrn_gnorm_w, hgrn_w_out, gla_w_in, gla_conv_w, gla_w_gate, gla_b_gate, gla_gnorm_w, gla_w_out, peer_w_q, peer_keys, peer_u, peer_v):
    bp, lp, d = x_prompt.shape
    bs, ls, _ = x_sample.shape
    t_p, t_s = bp * lp, bs * ls
    rows_s = ls // GRID_W
    stream = _Stream(t_p, ls, ROW_TILE)

    x = jnp.concatenate([x_prompt.reshape(t_p, d), x_sample.reshape(t_s, d)], axis=0)

    cvec = jnp.concatenate([c_ctx[None, :], c, jnp.zeros((MOD_ROWS - 1 - bs, d), F32)], axis=0)
    mod3 = _adaln(cvec, ada_w, ada_b).reshape(DEPTH * MOD_ROWS * 6, 1, d)

    p = jax.nn.softmax(hgrn_lb.astype(F32), axis=0)
    lower_bounds = jnp.cumsum(p, axis=0) - p[0:1]

    gla_n = gla_w_in.shape[2]
    gla_np = -(-gla_n // LANE) * LANE
    low_col_block = (QKV_B + D_MODEL) // LANE
    n_b = gla_w_in.shape[0]
    wg_pad = jnp.zeros((n_b, 2, LANE, H_B * DK_B), F32)
    wg_pad = wg_pad.at[:, 0, 0:GLA_RANK].set(gla_w_gate[:, 0]).at[:, 1, GLA_RANK:2 * GLA_RANK].set(gla_w_gate[:, 1])

    new_hgrn, new_gla = [], []
    for l in range(DEPTH):
        j = l // 2
        if l % 2 == 0:
            z = _norm_mod_matmul(x, norm1_w[l], mod3, stream, l, 0, 1, hgrn_w_in[j].astype(BF16), 512)
            lb = lower_bounds[j]
            of_p, ob_p, sf, sb = _hgrn_scan(z, lb, None, 0, bp, lp)
            init = (jnp.swapaxes(state_hgrn[:, j, 0], -1, -2), jnp.swapaxes(state_hgrn[:, j, 1], -1, -2))
            of_s, ob_s, _, _ = _hgrn_scan(z, lb, init, t_p, bs, ls)
            new_hgrn.append(jnp.stack([jnp.swapaxes(sf, -1, -2), jnp.swapaxes(sb, -1, -2)], axis=1))
            heads, dvh, og_blk = H_A, DV_A, 4
            gnw, w_out = hgrn_gnorm_w[j], hgrn_w_out[j]
        else:
            w_in = jnp.pad(gla_w_in[j], ((0, 0), (0, gla_np - gla_n))).astype(BF16)
            z = _norm_mod_matmul(x, norm1_w[l], mod3, stream, l, 0, 1, w_in, gla_np // 5 if gla_np % 5 == 0 else LANE)
            cw = gla_conv_w[j].reshape(9, QKV_B)
            y_p = _conv_silu(z, cw, 0, bp, 1, lp)
            y_s = _conv_silu(z, cw, t_p, bs, rows_s, GRID_W)
            wg, bg = wg_pad[j], gla_b_gate[j]
            of_p, ob_p, sf, sb = _gla_scan(y_p, z, wg, bg, None, 0, bp, lp, low_col_block)
            init = (jnp.swapaxes(state_gla[:, j, 0], -1, -2), jnp.swapaxes(state_gla[:, j, 1], -1, -2))
            of_s, ob_s, _, _ = _gla_scan(y_s, z, wg, bg, init, t_p, bs, ls, low_col_block)
            new_gla.append(jnp.stack([jnp.swapaxes(sf, -1, -2), jnp.swapaxes(sb, -1, -2)], axis=1))
            heads, dvh, og_blk = H_B, DV_B, 2
            gnw, w_out = gla_gnorm_w[j], gla_w_out[j]
        o_f = jnp.concatenate([of_p, of_s], axis=0)
        o_b = jnp.concatenate([ob_p, ob_s], axis=0)
        x = _outproj(o_f, o_b, z, og_blk, gnw, w_out.astype(BF16), x, mod3, stream, l, heads, dvh)

        q, h = _norm_mod_matmul(x, norm2_w[l], mod3, stream, l, 3, 4, peer_w_q[l].astype(BF16), 512, emit_h=True)
        idx, gates = _route(q, peer_keys[l])
        n_exp = peer_u.shape[1]
        uv3 = jnp.concatenate([peer_u[l].reshape(n_exp, D_TILES, LANE), peer_v[l].reshape(n_exp, D_TILES, LANE)], axis=1)
        x = _peer(idx.T, gates.T, h, x, mod3, stream, l, uv3)

    y = _rms_norm(x, final_norm_w)
    y_prompt = y[:t_p].reshape(bp, lp, d)
    y_sample = y[t_p:].reshape(bs, ls, d)
    return (y_prompt, y_sample, jnp.stack(new_hgrn, axis=1), jnp.stack(new_gla, axis=1))
```

```python
import functools

import jax
import jax.numpy as jnp
from jax import lax
from jax.experimental import pallas as pl
from jax.experimental.pallas import tpu as pltpu

F32 = jnp.float32
BF16 = jnp.bfloat16

D_MODEL = 1024
DEPTH = 4
GRID_W = 64
EPS = 1e-6
F_MIN = 1e-20
H_A, DK_A, DV_A = 8, 128, 128
H_B, DK_B, DV_B = 4, 128, 256
QKV_B = 2048
GLA_RANK = 16
GATE_NORM = 16.0
P_HEADS, N_KEYS, P_TOPK, P_HALF = 8, 128, 16, 128
N_SEL = P_HEADS * P_TOPK

V7X_VMEM_LIMIT = 48 * 1024 * 1024
LANE = 128
ROW_TILE = 512
CHUNK = 64
SUB = 16
MOD_ROWS = 16
NEG_INF = float("-inf")


def _cparams(sem):
    return pltpu.CompilerParams(dimension_semantics=sem, vmem_limit_bytes=V7X_VMEM_LIMIT)


def _silu(x):
    return x * jax.nn.sigmoid(x)


def _gelu(x):
    return 0.5 * x * (1.0 + lax.erf(x * (2.0 ** -0.5)))


def _dot(a, b):
    return jnp.dot(a, b, preferred_element_type=F32)


def _dot_nt(a, b):
    return lax.dot_general(a, b, (((1,), (1,)), ((), ())), preferred_element_type=F32)


def _dot_tn(a, b):
    return lax.dot_general(a, b, (((0,), (0,)), ((), ())), preferred_element_type=F32)


def _dot_01(m01, x):
    x1 = x.astype(BF16)
    r1 = x - x1.astype(F32)
    x2 = r1.astype(BF16)
    x3 = (r1 - x2.astype(F32)).astype(BF16)
    return _dot(m01, x1) + _dot(m01, x2) + _dot(m01, x3)


def _adaln_kernel(c_ref, w_ref, b_ref, o_ref):
    o_ref[0] = _dot(_silu(c_ref[...]), w_ref[0]) + b_ref[0]


def _adaln(cvec, ada_w, ada_b):
    tn = 1536
    n = 6 * D_MODEL
    return pl.pallas_call(
        _adaln_kernel,
        grid=(DEPTH, n // tn),
        in_specs=[
            pl.BlockSpec((MOD_ROWS, D_MODEL), lambda l, j: (0, 0)),
            pl.BlockSpec((1, D_MODEL, tn), lambda l, j: (l, 0, j)),
            pl.BlockSpec((1, 1, tn), lambda l, j: (l, 0, j)),
        ],
        out_specs=pl.BlockSpec((1, MOD_ROWS, tn), lambda l, j: (l, 0, j)),
        out_shape=jax.ShapeDtypeStruct((DEPTH, MOD_ROWS, n), F32),
        compiler_params=_cparams(("parallel", "parallel")),
        name="adaln",
    )(cvec, ada_w, ada_b.reshape(DEPTH, 1, n))


class _Stream:
    def __init__(self, t_prompt, seq_sample, tm):
        assert t_prompt % tm == 0 and seq_sample % tm == 0
        self.tm = tm
        self.n_prompt_blocks = t_prompt // tm
        self.blocks_per_sample = seq_sample // tm

    def mod_index(self, layer, k):
        npb, bps = self.n_prompt_blocks, self.blocks_per_sample

        def f(i, *_):
            row = jnp.where(i < npb, 0, 1 + (i - npb) // bps)
            return ((layer * MOD_ROWS + row) * 6 + k, 0, 0)

        return f


def _norm_mod(x, nw, shift, scale):
    ms = jnp.mean(x * x, axis=-1, keepdims=True)
    return (x * lax.rsqrt(ms + EPS) * nw) * (1.0 + scale) + shift


def _nmm_kernel(emit_h, x_ref, nw_ref, sh_ref, sc_ref, w_ref, *rest):
    if emit_h:
        o_ref, h_ref, hs_ref = rest
    else:
        o_ref, hs_ref = rest

    @pl.when(pl.program_id(1) == 0)
    def _():
        h = _norm_mod(x_ref[...], nw_ref[...], sh_ref[0], sc_ref[0])
        hs_ref[...] = h.astype(BF16)
        if emit_h:
            h_ref[...] = h

    o_ref[...] = _dot(hs_ref[...], w_ref[...])


def _norm_mod_matmul(x, nw, mod3, stream, layer, k_shift, k_scale, w_bf16, tn, emit_h=False):
    t, d = x.shape
    n = w_bf16.shape[1]
    tm = stream.tm
    assert n % tn == 0
    out_shape = [jax.ShapeDtypeStruct((t, n), F32)]
    out_specs = [pl.BlockSpec((tm, tn), lambda i, j: (i, j))]
    if emit_h:
        out_shape.append(jax.ShapeDtypeStruct((t, d), F32))
        out_specs.append(pl.BlockSpec((tm, d), lambda i, j: (i, 0)))
    res = pl.pallas_call(
        functools.partial(_nmm_kernel, emit_h),
        grid=(t // tm, n // tn),
        in_specs=[
            pl.BlockSpec((tm, d), lambda i, j: (i, 0)),
            pl.BlockSpec((1, d), lambda i, j: (0, 0)),
            pl.BlockSpec((1, 1, d), stream.mod_index(layer, k_shift)),
            pl.BlockSpec((1, 1, d), stream.mod_index(layer, k_scale)),
            pl.BlockSpec((d, tn), lambda i, j: (0, j)),
        ],
        out_specs=out_specs,
        out_shape=out_shape,
        scratch_shapes=[pltpu.VMEM((tm, d), BF16)],
        compiler_params=_cparams(("parallel", "arbitrary")),
        name="norm_mod_matmul",
    )(x, nw.reshape(1, d), mod3, mod3, w_bf16)
    return res if emit_h else res[0]


def _scan_direction(q, k, v, g, st_ref, sidx, rev):
    c = CHUNK
    row = lax.broadcasted_iota(jnp.int32, (c, c), 0)
    col = lax.broadcasted_iota(jnp.int32, (c, c), 1)
    cmat = jnp.where((col >= row) if rev else (col <= row), 1.0, 0.0).astype(BF16)
    cs = _dot_01(cmat, g)
    st = st_ref[sidx]
    o = _dot_nt(q * jnp.exp(cs), st)
    edge = cs[0:1] if rev else cs[c - 1:c]
    st_ref[sidx] = st * jnp.exp(edge) + _dot_tn(v, k * jnp.exp(edge - cs))

    srow = lax.broadcasted_iota(jnp.int32, (SUB, c), 0)
    scol = lax.broadcasted_iota(jnp.int32, (SUB, c), 1)
    att_rows = []
    for blk in range(c // SUB):
        r0 = blk * SUB
        q_i, k_i, cs_i = q[r0:r0 + SUB], k[r0:r0 + SUB], cs[r0:r0 + SUB]
        has_off = (blk < c // SUB - 1) if rev else (blk > 0)
        if has_off:
            ref = cs[r0 + SUB:r0 + SUB + 1] if rev else cs[r0 - 1:r0]
            w_i = cs_i - ref
            kt = k * jnp.exp(jnp.minimum(ref - cs, 0.0))
            off = _dot_nt(q_i * jnp.exp(w_i), kt)
            off_mask = (scol >= r0 + SUB) if rev else (scol < r0)
            a_i = jnp.where(off_mask, off, 0.0)
        else:
            w_i = cs_i
            a_i = jnp.zeros((SUB, c), F32)
        for s in range(SUB):
            e = jnp.exp(jnp.minimum(w_i - w_i[s:s + 1], 0.0))
            colv = jnp.sum(q_i * e * k_i[s:s + 1], axis=1, keepdims=True)
            keep = (srow <= s) if rev else (srow >= s)
            a_i = jnp.where((scol == r0 + s) & keep, colv, a_i)
        att_rows.append(a_i)
    att = jnp.concatenate(att_rows, axis=0)
    return o + _dot(att, v)


def _log_sigmoid(x):
    return jnp.minimum(x, 0.0) - jnp.log(1.0 + jnp.exp(-jnp.abs(x)))


SCAN_HPB = 4


def _scan_kernel(mode, zero_init, dk, dv, *refs):
    hpb = SCAN_HPB
    if mode == "hgrn":
        (qf, vf, ff, qb, vb, fb, lbf, lbb), rest = refs[:8], refs[8:]
    else:
        (qf, kf, vf, lowf, qb, kb, vb, lowb, wgf, wgb, bgf, bgb), rest = refs[:12], refs[12:]
    if not zero_init:
        (s0f, s0b), rest = rest[:2], rest[2:]
    of_ref, ob_ref, sf_ref, sb_ref, st_ref = rest
    ci = pl.program_id(2)

    @pl.when(ci == 0)
    def _():
        if zero_init:
            st_ref[...] = jnp.zeros_like(st_ref)
        else:
            for j in range(hpb):
                st_ref[j] = s0f[0, j]
                st_ref[hpb + j] = s0b[0, j]

    def prep(ks, q_ref, k_ref, f_ref, lb_ref, low_ref, wg_ref, bg_ref):
        if mode == "hgrn":
            lb = lb_ref[:, ks]
            fp = f_ref[:, ks]
            f = lb + (1.0 - lb) * jax.nn.sigmoid(fp)
            g = jnp.log(jnp.maximum(f, F_MIN))
            k = (1.0 - lb) * jax.nn.sigmoid(-fp)
            q = _silu(q_ref[:, ks])
        else:
            gk = _dot(low_ref[...], wg_ref[:, ks]) + bg_ref[:, ks]
            g = _log_sigmoid(gk) * (1.0 / GATE_NORM)
            k = k_ref[:, ks]
            q = q_ref[:, ks] * (DK_B ** -0.5)
        return q, k, g

    for j in range(hpb):
        ks, vs = slice(j * dk, (j + 1) * dk), slice(j * dv, (j + 1) * dv)
        if mode == "hgrn":
            q, k, g = prep(ks, qf, None, ff, lbf, None, None, None)
            of_ref[:, vs] = _scan_direction(q, k, vf[:, vs], g, st_ref, j, False)
            q, k, g = prep(ks, qb, None, fb, lbb, None, None, None)
            ob_ref[:, vs] = _scan_direction(q, k, vb[:, vs], g, st_ref, hpb + j, True)
        else:
            q, k, g = prep(ks, qf, kf, None, None, lowf, wgf, bgf)
            of_ref[:, vs] = _scan_direction(q, k, vf[:, vs], g, st_ref, j, False)
            q, k, g = prep(ks, qb, kb, None, None, lowb, wgb, bgb)
            ob_ref[:, vs] = _scan_direction(q, k, vb[:, vs], g, st_ref, hpb + j, True)

    @pl.when(ci == pl.num_programs(2) - 1)
    def _():
        for j in range(hpb):
            sf_ref[0, j] = st_ref[j]
            sb_ref[0, j] = st_ref[hpb + j]


def _scan(mode, arrays, consts, init_states, batch, seq, heads, dk, dv):
    n = seq // CHUNK
    hpb = SCAN_HPB
    assert heads % hpb == 0
    zero_init = init_states is None

    def chunk_of(c, dirn):
        return c if dirn == 0 else n - 1 - c

    in_specs, operands = [], []
    for dirn in (0, 1):
        for arr, width, row0, cols, per_head in arrays:
            assert row0 % CHUNK == 0 and (not per_head or cols[dirn] % hpb == 0)

            def imap(b, hg, c, dirn=dirn, rb0=row0 // CHUNK, col0=cols[dirn], per_head=per_head):
                return (rb0 + b * n + chunk_of(c, dirn), (col0 // hpb + hg) if per_head else col0)

            in_specs.append(pl.BlockSpec((CHUNK, width * hpb if per_head else width), imap))
            operands.append(arr)
    for arr, shape in consts:
        in_specs.append(pl.BlockSpec((shape[0], shape[1] * hpb), lambda b, hg, c: (0, hg)))
        operands.append(arr)
    s_spec = pl.BlockSpec((1, hpb, dv, dk), lambda b, hg, c: (b, hg, 0, 0))
    if not zero_init:
        for s0 in init_states:
            in_specs.append(s_spec)
            operands.append(s0)
    o_specs = [pl.BlockSpec((CHUNK, dv * hpb), lambda b, hg, c, dirn=dirn: (b * n + chunk_of(c, dirn), hg))
               for dirn in (0, 1)]
    return pl.pallas_call(
        functools.partial(_scan_kernel, mode, zero_init, dk, dv),
        grid=(batch, heads // hpb, n),
        in_specs=in_specs,
        out_specs=o_specs + [s_spec, s_spec],
        out_shape=[
            jax.ShapeDtypeStruct((batch * seq, heads * dv), F32),
            jax.ShapeDtypeStruct((batch * seq, heads * dv), F32),
            jax.ShapeDtypeStruct((batch, heads, dv, dk), F32),
            jax.ShapeDtypeStruct((batch, heads, dv, dk), F32),
        ],
        scratch_shapes=[pltpu.VMEM((2 * hpb, dv, dk), F32)],
        compiler_params=_cparams(("parallel", "parallel", "arbitrary")),
        name="scan_" + mode,
    )(*operands)


def _hgrn_scan(z, lb, init_states, row0, batch, seq):
    nh = H_A
    arrays = [
        (z, DK_A, row0, (0, 0), True),
        (z, DV_A, row0, (nh, nh), True),
        (z, DK_A, row0, (2 * nh, 3 * nh), True),
    ]
    consts = [(lb[0:1], (1, DK_A)), (lb[1:2], (1, DK_A))]
    return _scan("hgrn", arrays, consts, init_states, batch, seq, H_A, DK_A, DV_A)


def _gla_scan(y, z, wg, bg, init_states, row0, batch, seq, low_col_block):
    v0 = 2 * H_B * DK_B // DV_B
    arrays = [
        (y, DK_B, 0, (0, 0), True),
        (y, DK_B, 0, (H_B, H_B), True),
        (y, DV_B, 0, (v0, v0), True),
        (z, LANE, row0, (low_col_block, low_col_block), False),
    ]
    consts = [(wg[0], (LANE, DK_B)), (wg[1], (LANE, DK_B)), (bg[0:1], (1, DK_B)), (bg[1:2], (1, DK_B))]
    return _scan("gla", arrays, consts, init_states, batch, seq, H_B, DK_B, DV_B)


def _conv_kernel(rows, width, x_ref, w_ref, o_ref):
    seq = rows * width
    x = x_ref[...]
    assert width & (width - 1) == 0
    l = lax.broadcasted_iota(jnp.int32, (seq, 1), 0)
    r, c = l >> (width.bit_length() - 1), l & (width - 1)
    acc = jnp.zeros_like(x)
    for di in range(3):
        if rows == 1 and di != 1:
            continue
        for dj in range(3):
            k = (di - 1) * width + (dj - 1)
            xs = x if k == 0 else pltpu.roll(x, (-k) % seq, 0)
            ok = (r + (di - 1) >= 0) & (r + (di - 1) < rows) & (c + (dj - 1) >= 0) & (c + (dj - 1) < width)
            acc = acc + jnp.where(ok, xs, 0.0) * w_ref[3 * di + dj:3 * di + dj + 1, :]
    o_ref[...] = _silu(acc)


def _conv_silu(z, conv_w9, row0, batch, rows, width, tc=256):
    seq = rows * width
    sb0 = row0 // seq
    assert row0 % seq == 0
    return pl.pallas_call(
        functools.partial(_conv_kernel, rows, width),
        grid=(batch, QKV_B // tc),
        in_specs=[
            pl.BlockSpec((seq, tc), lambda b, j: (sb0 + b, j)),
            pl.BlockSpec((9, tc), lambda b, j: (0, j)),
        ],
        out_specs=pl.BlockSpec((seq, tc), lambda b, j: (b, j)),
        out_shape=jax.ShapeDtypeStruct((batch * seq, QKV_B), F32),
        compiler_params=_cparams(("parallel", "parallel")),
        name="conv_silu",
    )(z, conv_w9)


def _outproj_kernel(heads, dvh, of_ref, ob_ref, og_ref, gw_ref, w_ref, x_ref, g_ref, o_ref):
    o = of_ref[...] + ob_ref[...]
    og = og_ref[...]
    gw = gw_ref[...]
    parts = []
    for h in range(heads):
        oh = o[:, h * dvh:(h + 1) * dvh]
        ms = jnp.mean(oh * oh, axis=-1, keepdims=True)
        parts.append(oh * lax.rsqrt(ms + EPS) * gw)
    y = jnp.concatenate(parts, axis=1) * _silu(og)
    o_ref[...] = x_ref[...] + g_ref[0] * _dot(y.astype(BF16), w_ref[...])


def _outproj(o_f, o_b, z, og_col_block, gnorm_w, w_out_bf16, x, mod3, stream, layer, heads, dvh):
    t, d = x.shape
    tm = 256
    st = _Stream(stream.n_prompt_blocks * stream.tm, stream.blocks_per_sample * stream.tm, tm)
    return pl.pallas_call(
        functools.partial(_outproj_kernel, heads, dvh),
        grid=(t // tm,),
        in_specs=[
            pl.BlockSpec((tm, d), lambda i: (i, 0)),
            pl.BlockSpec((tm, d), lambda i: (i, 0)),
            pl.BlockSpec((tm, d), lambda i: (i, og_col_block)),
            pl.BlockSpec((1, dvh), lambda i: (0, 0)),
            pl.BlockSpec((d, d), lambda i: (0, 0)),
            pl.BlockSpec((tm, d), lambda i: (i, 0)),
            pl.BlockSpec((1, 1, d), st.mod_index(layer, 2)),
        ],
        out_specs=pl.BlockSpec((tm, d), lambda i: (i, 0)),
        out_shape=jax.ShapeDtypeStruct((t, d), F32),
        compiler_params=_cparams(("parallel",)),
        name="outproj",
    )(o_f, o_b, z, gnorm_w.reshape(1, dvh), w_out_bf16, x, mod3)


def _top_rows(s, k):
    n, tm = s.shape
    rid = lax.broadcasted_iota(jnp.int32, (n, tm), 0).astype(F32)
    kid = lax.broadcasted_iota(jnp.int32, (k, tm), 0)

    def body(j, carry):
        s, vals, ids = carry
        m = jnp.max(s, axis=0, keepdims=True)
        pos = jnp.min(jnp.where(s == m, rid, float(n)), axis=0, keepdims=True)
        vals = jnp.where(kid == j, m, vals)
        ids = jnp.where(kid == j, pos, ids)
        s = jnp.where(rid == pos, NEG_INF, s)
        return s, vals, ids

    _, vals, ids = lax.fori_loop(0, k, body, (s, jnp.zeros((k, tm), F32), jnp.zeros((k, tm), F32)))
    return vals, ids


_PAIR_COUNTS = [P_TOPK // (a + 1) for a in range(P_TOPK)]
_N_PAIRS = sum(_PAIR_COUNTS)
_N_PAIR_ROWS = -(-_N_PAIRS // 8) * 8


def _route_kernel(q_ref, keys_ref, idx_ref, gate_ref, cand_ref, cid_ref):
    tm = q_ref.shape[0]
    vals, ids = [], []
    for p in range(2):
        s = _dot_nt(keys_ref[0, p], q_ref[:, p * P_HALF:(p + 1) * P_HALF])
        v, i = _top_rows(s, P_TOPK)
        vals.append(v)
        ids.append(i)
    cand_ref[...] = jnp.full(cand_ref.shape, NEG_INF, F32)
    cid_ref[...] = jnp.zeros(cid_ref.shape, F32)
    r0 = 0
    for a, nb in enumerate(_PAIR_COUNTS):
        cand_ref[r0:r0 + nb, :] = vals[0][a:a + 1] + vals[1][0:nb]
        cid_ref[r0:r0 + nb, :] = ids[0][a:a + 1] * float(N_KEYS) + ids[1][0:nb]
        r0 += nb
    cand = cand_ref[...]
    cid = cid_ref[...]
    n = cand.shape[0]
    rid = lax.broadcasted_iota(jnp.int32, (n, tm), 0).astype(F32)
    kid = lax.broadcasted_iota(jnp.int32, (P_TOPK, tm), 0)

    def body(j, carry):
        cand, top, idx = carry
        m = jnp.max(cand, axis=0, keepdims=True)
        pos = jnp.min(jnp.where(cand == m, rid, float(n)), axis=0, keepdims=True)
        sel = rid == pos
        e = jnp.sum(jnp.where(sel, cid, 0.0), axis=0, keepdims=True)
        top = jnp.where(kid == j, m, top)
        idx = jnp.where(kid == j, e, idx)
        return jnp.where(sel, NEG_INF, cand), top, idx

    _, top, idx = lax.fori_loop(
        0, P_TOPK, body, (cand, jnp.zeros((P_TOPK, tm), F32), jnp.zeros((P_TOPK, tm), F32)))
    ex = jnp.exp(top - top[0:1])
    gate_ref[...] = ex / jnp.sum(ex, axis=0, keepdims=True)
    idx_ref[...] = idx.astype(jnp.int32)


def _route(q, keys):
    t = q.shape[0]
    tm = 256
    return pl.pallas_call(
        _route_kernel,
        grid=(t // tm, P_HEADS),
        in_specs=[
            pl.BlockSpec((tm, 2 * P_HALF), lambda i, h: (i, h)),
            pl.BlockSpec((1, 2, N_KEYS, P_HALF), lambda i, h: (h, 0, 0, 0)),
        ],
        out_specs=[
            pl.BlockSpec((P_TOPK, tm), lambda i, h: (h, i)),
            pl.BlockSpec((P_TOPK, tm), lambda i, h: (h, i)),
        ],
        out_shape=[
            jax.ShapeDtypeStruct((N_SEL, t), jnp.int32),
            jax.ShapeDtypeStruct((N_SEL, t), F32),
        ],
        scratch_shapes=[pltpu.VMEM((_N_PAIR_ROWS, tm), F32), pltpu.VMEM((_N_PAIR_ROWS, tm), F32)],
        compiler_params=_cparams(("parallel", "parallel")),
        name="peer_route",
    )(q, keys)


PEER_TB = 128
PEER_NBUF = 8
SUBLANES = 8
D_TILES = D_MODEL // LANE


def _sublane_sums(p):
    s = lax.broadcasted_iota(jnp.int32, p.shape[2:], 0)
    sel = s < 4
    a, b = p[:, 0:4], p[:, 4:8]
    f = jnp.where(sel, a, b) + pltpu.roll(jnp.where(sel, b, a), 4, 2)
    sel = (s & 2) == 0
    a, b = f[:, 0:2], f[:, 2:4]
    g = jnp.where(sel, a + pltpu.roll(a, 6, 2), b + pltpu.roll(b, 2, 2))
    sel = (s & 1) == 0
    a, b = g[:, 0], g[:, 1]
    return jnp.where(sel, a + pltpu.roll(a, 7, 1), b + pltpu.roll(b, 1, 1))


def _peer_kernel(idx_ref, idx_next_ref, h_ref, gate_ref, x_ref, g2_ref, uv_hbm, o_ref, *scratch):
    bufs, cb_ref, sem = scratch[:PEER_NBUF], scratch[PEER_NBUF], scratch[PEER_NBUF + 1]
    step, nsteps = pl.program_id(0), pl.num_programs(0)

    def start_token(ids_ref, t, slot):
        for e in range(N_SEL):
            pltpu.make_async_copy(uv_hbm.at[ids_ref[t, e]], bufs[slot].at[e], sem.at[slot]).start(priority=e % 2)

    def wait_token(slot):
        pltpu.make_async_copy(uv_hbm.at[pl.ds(0, N_SEL)], bufs[slot], sem.at[slot]).wait()

    eye = lax.broadcasted_iota(jnp.int32, (N_SEL, N_SEL), 0) == lax.broadcasted_iota(jnp.int32, (N_SEL, N_SEL), 1)
    g2 = g2_ref[0]

    def compute(t, slot):
        buf = bufs[slot]
        prod = buf[:, 0:SUBLANES, :] * h_ref[t]
        rows = _sublane_sums(prod.reshape(N_SEL // 8, 8, SUBLANES, LANE))
        s = jnp.sum(rows.reshape(N_SEL, LANE), axis=1, keepdims=True)
        gcol = jnp.sum(jnp.where(eye, gate_ref[pl.ds(t, 1), :], 0.0), axis=1, keepdims=True)
        cb_ref[...] = jnp.broadcast_to(gcol * _gelu(s), (N_SEL, LANE))
        accs = [jnp.zeros((SUBLANES, LANE), F32) for _ in range(4)]
        for e in range(N_SEL):
            accs[e % 4] = accs[e % 4] + cb_ref[pl.ds(e, 1), :] * buf[e, SUBLANES:2 * SUBLANES, :]
        o_ref[t] = x_ref[t] + g2 * ((accs[0] + accs[1]) + (accs[2] + accs[3]))

    ahead = PEER_NBUF - 1

    @pl.when(step == 0)
    def _():
        for t in range(ahead):
            start_token(idx_ref, t, t)

    def group(t0, last):
        for k in range(PEER_NBUF):
            wait_token(k)
            nxt, slot = t0 + k + ahead, (k + ahead) % PEER_NBUF
            if not last:
                start_token(idx_ref, nxt, slot)
            elif k + ahead < PEER_NBUF:
                start_token(idx_ref, PEER_TB - PEER_NBUF + k + ahead, slot)
            else:
                @pl.when(step + 1 < nsteps)
                def _():
                    start_token(idx_next_ref, k + ahead - PEER_NBUF, slot)
            compute(t0 + k, k)

    def body(i, _):
        group(PEER_NBUF * i, False)
        return 0

    lax.fori_loop(0, PEER_TB // PEER_NBUF - 1, body, 0)
    group(PEER_TB - PEER_NBUF, True)


def _peer(idx_t, gate_t, h, x, mod3, stream, layer, uv3):
    t, d = x.shape
    tb = PEER_TB
    st = _Stream(stream.n_prompt_blocks * stream.tm, stream.blocks_per_sample * stream.tm, tb)
    tok_spec = pl.BlockSpec((tb, D_TILES, LANE), lambda i: (i, 0, 0))
    last_block = t // tb - 1
    out = pl.pallas_call(
        _peer_kernel,
        grid=(t // tb,),
        in_specs=[
            pl.BlockSpec((tb, N_SEL), lambda i: (i, 0), memory_space=pltpu.SMEM),
            pl.BlockSpec((tb, N_SEL), lambda i: (jnp.minimum(i + 1, last_block), 0), memory_space=pltpu.SMEM),
            tok_spec,
            pl.BlockSpec((tb, N_SEL), lambda i: (i, 0)),
            tok_spec,
            pl.BlockSpec((1, D_TILES, LANE), st.mod_index(layer, 5)),
            pl.BlockSpec(memory_space=pl.ANY),
        ],
        out_specs=tok_spec,
        out_shape=jax.ShapeDtypeStruct((t, D_TILES, LANE), F32),
        scratch_shapes=[pltpu.VMEM((N_SEL, 2 * SUBLANES, LANE), F32) for _ in range(PEER_NBUF)] + [
            pltpu.VMEM((N_SEL, LANE), F32),
            pltpu.SemaphoreType.DMA((PEER_NBUF,)),
        ],
        compiler_params=_cparams(("arbitrary",)),
        name="peer_experts",
    )(idx_t, idx_t, h.reshape(t, D_TILES, LANE), gate_t, x.reshape(t, D_TILES, LANE),
      mod3.reshape(-1, D_TILES, LANE), uv3)
    return out.reshape(t, d)


def _rms_kernel(x_ref, w_ref, o_ref):
    x = x_ref[...]
    ms = jnp.mean(x * x, axis=-1, keepdims=True)
    o_ref[...] = x * lax.rsqrt(ms + EPS) * w_ref[...]


def _rms_norm(x, w):
    t, d = x.shape
    tm = 512
    return pl.pallas_call(
        _rms_kernel,
        grid=(t // tm,),
        in_specs=[pl.BlockSpec((tm, d), lambda i: (i, 0)), pl.BlockSpec((1, d), lambda i: (0, 0))],
        out_specs=pl.BlockSpec((tm, d), lambda i: (i, 0)),
        out_shape=jax.ShapeDtypeStruct((t, d), F32),
        compiler_params=_cparams(("parallel",)),
        name="final_norm",
    )(x, w.reshape(1, d))


def kernel(x_prompt, x_sample, state_hgrn, state_gla, c, c_ctx, ada_w, ada_b, norm1_w, norm2_w, final_norm_w, hgrn_w_in, hgrn_lb, hgrn_gnorm_w, hgrn_w_out, gla_w_in, gla_conv_w, gla_w_gate, gla_b_gate, gla_gnorm_w, gla_w_out, peer_w_q, peer_keys, peer_u, peer_v):
    bp, lp, d = x_prompt.shape
    bs, ls, _ = x_sample.shape
    t_p, t_s = bp * lp, bs * ls
    rows_s = ls // GRID_W
    stream = _Stream(t_p, ls, ROW_TILE)

    x = jnp.concatenate([x_prompt.reshape(t_p, d), x_sample.reshape(t_s, d)], axis=0)

    cvec = jnp.concatenate([c_ctx[None, :], c, jnp.zeros((MOD_ROWS - 1 - bs, d), F32)], axis=0)
    mod3 = _adaln(cvec, ada_w, ada_b).reshape(DEPTH * MOD_ROWS * 6, 1, d)

    p = jax.nn.softmax(hgrn_lb.astype(F32), axis=0)
    lower_bounds = jnp.cumsum(p, axis=0) - p[0:1]

    gla_n = gla_w_in.shape[2]
    gla_np = -(-gla_n // LANE) * LANE
    low_col_block = (QKV_B + D_MODEL) // LANE
    n_b = gla_w_in.shape[0]
    wg_pad = jnp.zeros((n_b, 2, LANE, H_B * DK_B), F32)
    wg_pad = wg_pad.at[:, 0, 0:GLA_RANK].set(gla_w_gate[:, 0]).at[:, 1, GLA_RANK:2 * GLA_RANK].set(gla_w_gate[:, 1])

    new_hgrn, new_gla = [], []
    for l in range(DEPTH):
        j = l // 2
        if l % 2 == 0:
            z = _norm_mod_matmul(x, norm1_w[l], mod3, stream, l, 0, 1, hgrn_w_in[j].astype(BF16), 512)
            lb = lower_bounds[j]
            of_p, ob_p, sf, sb = _hgrn_scan(z, lb, None, 0, bp, lp)
            init = (jnp.swapaxes(state_hgrn[:, j, 0], -1, -2), jnp.swapaxes(state_hgrn[:, j, 1], -1, -2))
            of_s, ob_s, _, _ = _hgrn_scan(z, lb, init, t_p, bs, ls)
            new_hgrn.append(jnp.stack([jnp.swapaxes(sf, -1, -2), jnp.swapaxes(sb, -1, -2)], axis=1))
            heads, dvh, og_blk = H_A, DV_A, 4
            gnw, w_out = hgrn_gnorm_w[j], hgrn_w_out[j]
        else:
            w_in = jnp.pad(gla_w_in[j], ((0, 0), (0, gla_np - gla_n))).astype(BF16)
            z = _norm_mod_matmul(x, norm1_w[l], mod3, stream, l, 0, 1, w_in, gla_np // 5 if gla_np % 5 == 0 else LANE)
            cw = gla_conv_w[j].reshape(9, QKV_B)
            y_p = _conv_silu(z, cw, 0, bp, 1, lp)
            y_s = _conv_silu(z, cw, t_p, bs, rows_s, GRID_W)
            wg, bg = wg_pad[j], gla_b_gate[j]
            of_p, ob_p, sf, sb = _gla_scan(y_p, z, wg, bg, None, 0, bp, lp, low_col_block)
            init = (jnp.swapaxes(state_gla[:, j, 0], -1, -2), jnp.swapaxes(state_gla[:, j, 1], -1, -2))
            of_s, ob_s, _, _ = _gla_scan(y_s, z, wg, bg, init, t_p, bs, ls, low_col_block)
            new_gla.append(jnp.stack([jnp.swapaxes(sf, -1, -2), jnp.swapaxes(sb, -1, -2)], axis=1))
            heads, dvh, og_blk = H_B, DV_B, 2
            gnw, w_out = gla_gnorm_w[j], gla_w_out[j]
        o_f = jnp.concatenate([of_p, of_s], axis=0)
        o_b = jnp.concatenate([ob_p, ob_s], axis=0)
        x = _outproj(o_f, o_b, z, og_blk, gnw, w_out.astype(BF16), x, mod3, stream, l, heads, dvh)

        q, h = _norm_mod_matmul(x, norm2_w[l], mod3, stream, l, 3, 4, peer_w_q[l].astype(BF16), 512, emit_h=True)
        idx, gates = _route(q, peer_keys[l])
        n_exp = peer_u.shape[1]
        uv3 = jnp.concatenate([peer_u[l].reshape(n_exp, D_TILES, LANE), peer_v[l].reshape(n_exp, D_TILES, LANE)], axis=1)
        x = _peer(idx.T, gates.T, h, x, mod3, stream, l, uv3)

    y = _rms_norm(x, final_norm_w)
    y_prompt = y[:t_p].reshape(bp, lp, d)
    y_sample = y[t_p:].reshape(bs, ls, d)
    return (y_prompt, y_sample, jnp.stack(new_hgrn, axis=1), jnp.stack(new_gla, axis=1))
```

```python
import functools

import jax
import jax.numpy as jnp
from jax import lax
from jax.experimental import pallas as pl
from jax.experimental.pallas import tpu as pltpu

F32 = jnp.float32
BF16 = jnp.bfloat16

D_MODEL = 1024
DEPTH = 4
GRID_W = 64
EPS = 1e-6
F_MIN = 1e-20
H_A, DK_A, DV_A = 8, 128, 128
H_B, DK_B, DV_B = 4, 128, 256
QKV_B = 2048
GLA_RANK = 16
GATE_NORM = 16.0
P_HEADS, N_KEYS, P_TOPK, P_HALF = 8, 128, 16, 128
N_SEL = P_HEADS * P_TOPK

V7X_VMEM_LIMIT = 48 * 1024 * 1024
LANE = 128
ROW_TILE = 512
CHUNK = 64
SUB = 16
MOD_ROWS = 16
NEG_INF = float("-inf")


def _cparams(sem):
    return pltpu.CompilerParams(dimension_semantics=sem, vmem_limit_bytes=V7X_VMEM_LIMIT)


def _silu(x):
    return x * jax.nn.sigmoid(x)


def _gelu(x):
    return 0.5 * x * (1.0 + lax.erf(x * (2.0 ** -0.5)))


def _dot(a, b):
    return jnp.dot(a, b, preferred_element_type=F32)


def _dot_nt(a, b):
    return lax.dot_general(a, b, (((1,), (1,)), ((), ())), preferred_element_type=F32)


def _dot_tn(a, b):
    return lax.dot_general(a, b, (((0,), (0,)), ((), ())), preferred_element_type=F32)


def _dot_01(m01, x):
    x1 = x.astype(BF16)
    r1 = x - x1.astype(F32)
    x2 = r1.astype(BF16)
    x3 = (r1 - x2.astype(F32)).astype(BF16)
    return _dot(m01, x1) + _dot(m01, x2) + _dot(m01, x3)


def _adaln_kernel(c_ref, w_ref, b_ref, o_ref):
    o_ref[0] = _dot(_silu(c_ref[...]), w_ref[0]) + b_ref[0]


def _adaln(cvec, ada_w, ada_b):
    tn = 1536
    n = 6 * D_MODEL
    return pl.pallas_call(
        _adaln_kernel,
        grid=(DEPTH, n // tn),
        in_specs=[
            pl.BlockSpec((MOD_ROWS, D_MODEL), lambda l, j: (0, 0)),
            pl.BlockSpec((1, D_MODEL, tn), lambda l, j: (l, 0, j)),
            pl.BlockSpec((1, 1, tn), lambda l, j: (l, 0, j)),
        ],
        out_specs=pl.BlockSpec((1, MOD_ROWS, tn), lambda l, j: (l, 0, j)),
        out_shape=jax.ShapeDtypeStruct((DEPTH, MOD_ROWS, n), F32),
        compiler_params=_cparams(("parallel", "parallel")),
        name="adaln",
    )(cvec, ada_w, ada_b.reshape(DEPTH, 1, n))


class _Stream:
    def __init__(self, t_prompt, seq_sample, tm):
        assert t_prompt % tm == 0 and seq_sample % tm == 0
        self.tm = tm
        self.n_prompt_blocks = t_prompt // tm
        self.blocks_per_sample = seq_sample // tm

    def mod_index(self, layer, k):
        npb, bps = self.n_prompt_blocks, self.blocks_per_sample

        def f(i, *_):
            row = jnp.where(i < npb, 0, 1 + (i - npb) // bps)
            return ((layer * MOD_ROWS + row) * 6 + k, 0, 0)

        return f


def _norm_mod(x, nw, shift, scale):
    ms = jnp.mean(x * x, axis=-1, keepdims=True)
    return (x * lax.rsqrt(ms + EPS) * nw) * (1.0 + scale) + shift


def _nmm_kernel(emit_h, x_ref, nw_ref, sh_ref, sc_ref, w_ref, *rest):
    if emit_h:
        o_ref, h_ref, hs_ref = rest
    else:
        o_ref, hs_ref = rest

    @pl.when(pl.program_id(1) == 0)
    def _():
        h = _norm_mod(x_ref[...], nw_ref[...], sh_ref[0], sc_ref[0])
        hs_ref[...] = h.astype(BF16)
        if emit_h:
            h_ref[...] = h

    o_ref[...] = _dot(hs_ref[...], w_ref[...])


def _norm_mod_matmul(x, nw, mod3, stream, layer, k_shift, k_scale, w_bf16, tn, emit_h=False):
    t, d = x.shape
    n = w_bf16.shape[1]
    tm = stream.tm
    assert n % tn == 0
    out_shape = [jax.ShapeDtypeStruct((t, n), F32)]
    out_specs = [pl.BlockSpec((tm, tn), lambda i, j: (i, j))]
    if emit_h:
        out_shape.append(jax.ShapeDtypeStruct((t, d), F32))
        out_specs.append(pl.BlockSpec((tm, d), lambda i, j: (i, 0)))
    res = pl.pallas_call(
        functools.partial(_nmm_kernel, emit_h),
        grid=(t // tm, n // tn),
        in_specs=[
            pl.BlockSpec((tm, d), lambda i, j: (i, 0)),
            pl.BlockSpec((1, d), lambda i, j: (0, 0)),
            pl.BlockSpec((1, 1, d), stream.mod_index(layer, k_shift)),
            pl.BlockSpec((1, 1, d), stream.mod_index(layer, k_scale)),
            pl.BlockSpec((d, tn), lambda i, j: (0, j)),
        ],
        out_specs=out_specs,
        out_shape=out_shape,
        scratch_shapes=[pltpu.VMEM((tm, d), BF16)],
        compiler_params=_cparams(("parallel", "arbitrary")),
        name="norm_mod_matmul",
    )(x, nw.reshape(1, d), mod3, mod3, w_bf16)
    return res if emit_h else res[0]


def _scan_direction(q, k, v, g, st_ref, sidx, rev):
    c = CHUNK
    row = lax.broadcasted_iota(jnp.int32, (c, c), 0)
    col = lax.broadcasted_iota(jnp.int32, (c, c), 1)
    cmat = jnp.where((col >= row) if rev else (col <= row), 1.0, 0.0).astype(BF16)
    cs = _dot_01(cmat, g)
    st = st_ref[sidx]
    o = _dot_nt(q * jnp.exp(cs), st)
    edge = cs[0:1] if rev else cs[c - 1:c]
    st_ref[sidx] = st * jnp.exp(edge) + _dot_tn(v, k * jnp.exp(edge - cs))

    srow = lax.broadcasted_iota(jnp.int32, (SUB, c), 0)
    scol = lax.broadcasted_iota(jnp.int32, (SUB, c), 1)
    att_rows = []
    for blk in range(c // SUB):
        r0 = blk * SUB
        q_i, k_i, cs_i = q[r0:r0 + SUB], k[r0:r0 + SUB], cs[r0:r0 + SUB]
        has_off = (blk < c // SUB - 1) if rev else (blk > 0)
        if has_off:
            ref = cs[r0 + SUB:r0 + SUB + 1] if rev else cs[r0 - 1:r0]
            w_i = cs_i - ref
            kt = k * jnp.exp(jnp.minimum(ref - cs, 0.0))
            off = _dot_nt(q_i * jnp.exp(w_i), kt)
            off_mask = (scol >= r0 + SUB) if rev else (scol < r0)
            a_i = jnp.where(off_mask, off, 0.0)
        else:
            w_i = cs_i
            a_i = jnp.zeros((SUB, c), F32)
        for s in range(SUB):
            e = jnp.exp(jnp.minimum(w_i - w_i[s:s + 1], 0.0))
            colv = jnp.sum(q_i * e * k_i[s:s + 1], axis=1, keepdims=True)
            keep = (srow <= s) if rev else (srow >= s)
            a_i = jnp.where((scol == r0 + s) & keep, colv, a_i)
        att_rows.append(a_i)
    att = jnp.concatenate(att_rows, axis=0)
    return o + _dot(att, v)


def _log_sigmoid(x):
    return jnp.minimum(x, 0.0) - jnp.log(1.0 + jnp.exp(-jnp.abs(x)))


SCAN_HPB = 4


def _scan_kernel(mode, zero_init, n_aliased, dk, dv, *refs):
    hpb = SCAN_HPB
    if mode == "hgrn":
        (qf, vf, ff, qb, vb, fb, lbf, lbb), rest = refs[:8], refs[8:]
    else:
        (qf, kf, vf, lowf, qb, kb, vb, lowb, wgf, wgb, bgf, bgb), rest = refs[:12], refs[12:]
    if not zero_init:
        (s0f, s0b), rest = rest[:2], rest[2:]
    rest = rest[n_aliased:]
    of_ref, ob_ref, sf_ref, sb_ref, st_ref = rest
    ci = pl.program_id(2)

    @pl.when(ci == 0)
    def _():
        if zero_init:
            st_ref[...] = jnp.zeros_like(st_ref)
        else:
            for j in range(hpb):
                st_ref[j] = s0f[0, j]
                st_ref[hpb + j] = s0b[0, j]

    def prep(ks, q_ref, k_ref, f_ref, lb_ref, low_ref, wg_ref, bg_ref):
        if mode == "hgrn":
            lb = lb_ref[:, ks]
            fp = f_ref[:, ks]
            f = lb + (1.0 - lb) * jax.nn.sigmoid(fp)
            g = jnp.log(jnp.maximum(f, F_MIN))
            k = (1.0 - lb) * jax.nn.sigmoid(-fp)
            q = _silu(q_ref[:, ks])
        else:
            gk = _dot(low_ref[...], wg_ref[:, ks]) + bg_ref[:, ks]
            g = _log_sigmoid(gk) * (1.0 / GATE_NORM)
            k = k_ref[:, ks]
            q = q_ref[:, ks] * (DK_B ** -0.5)
        return q, k, g

    for j in range(hpb):
        ks, vs = slice(j * dk, (j + 1) * dk), slice(j * dv, (j + 1) * dv)
        if mode == "hgrn":
            q, k, g = prep(ks, qf, None, ff, lbf, None, None, None)
            of_ref[:, vs] = _scan_direction(q, k, vf[:, vs], g, st_ref, j, False)
            q, k, g = prep(ks, qb, None, fb, lbb, None, None, None)
            ob_ref[:, vs] = _scan_direction(q, k, vb[:, vs], g, st_ref, hpb + j, True)
        else:
            q, k, g = prep(ks, qf, kf, None, None, lowf, wgf, bgf)
            of_ref[:, vs] = _scan_direction(q, k, vf[:, vs], g, st_ref, j, False)
            q, k, g = prep(ks, qb, kb, None, None, lowb, wgb, bgb)
            ob_ref[:, vs] = _scan_direction(q, k, vb[:, vs], g, st_ref, hpb + j, True)

    @pl.when(ci == pl.num_programs(2) - 1)
    def _():
        for j in range(hpb):
            sf_ref[0, j] = st_ref[j]
            sb_ref[0, j] = st_ref[hpb + j]


def _scan(mode, arrays, consts, init_states, batch, seq, heads, dk, dv, out_rows, out_row0, prev_out):
    n = seq // CHUNK
    hpb = SCAN_HPB
    assert heads % hpb == 0 and out_row0 % CHUNK == 0
    zero_init = init_states is None
    orb0 = out_row0 // CHUNK

    def chunk_of(c, dirn):
        return c if dirn == 0 else n - 1 - c

    in_specs, operands = [], []
    for dirn in (0, 1):
        for arr, width, row0, cols, per_head in arrays:
            assert row0 % CHUNK == 0 and (not per_head or cols[dirn] % hpb == 0)

            def imap(b, hg, c, dirn=dirn, rb0=row0 // CHUNK, col0=cols[dirn], per_head=per_head):
                return (rb0 + b * n + chunk_of(c, dirn), (col0 // hpb + hg) if per_head else col0)

            in_specs.append(pl.BlockSpec((CHUNK, width * hpb if per_head else width), imap))
            operands.append(arr)
    for arr, shape in consts:
        in_specs.append(pl.BlockSpec((shape[0], shape[1] * hpb), lambda b, hg, c: (0, hg)))
        operands.append(arr)
    s_spec = pl.BlockSpec((1, hpb, dv, dk), lambda b, hg, c: (b, hg, 0, 0))
    if not zero_init:
        for s0 in init_states:
            in_specs.append(s_spec)
            operands.append(s0)
    aliases = {}
    if prev_out is not None:
        for k, arr in enumerate(prev_out):
            aliases[len(operands)] = k
            in_specs.append(pl.BlockSpec(memory_space=pl.ANY))
            operands.append(arr)
    o_specs = [pl.BlockSpec((CHUNK, dv * hpb), lambda b, hg, c, dirn=dirn: (orb0 + b * n + chunk_of(c, dirn), hg))
               for dirn in (0, 1)]
    return pl.pallas_call(
        functools.partial(_scan_kernel, mode, zero_init, len(aliases), dk, dv),
        grid=(batch, heads // hpb, n),
        in_specs=in_specs,
        out_specs=o_specs + [s_spec, s_spec],
        out_shape=[
            jax.ShapeDtypeStruct((out_rows, heads * dv), F32),
            jax.ShapeDtypeStruct((out_rows, heads * dv), F32),
            jax.ShapeDtypeStruct((batch, heads, dv, dk), F32),
            jax.ShapeDtypeStruct((batch, heads, dv, dk), F32),
        ],
        scratch_shapes=[pltpu.VMEM((2 * hpb, dv, dk), F32)],
        input_output_aliases=aliases,
        compiler_params=_cparams(("parallel", "parallel", "arbitrary")),
        name="scan_" + mode,
    )(*operands)


def _hgrn_scan(z, lb, init_states, row0, batch, seq, prev_out=None):
    nh = H_A
    arrays = [
        (z, DK_A, row0, (0, 0), True),
        (z, DV_A, row0, (nh, nh), True),
        (z, DK_A, row0, (2 * nh, 3 * nh), True),
    ]
    consts = [(lb[0:1], (1, DK_A)), (lb[1:2], (1, DK_A))]
    return _scan("hgrn", arrays, consts, init_states, batch, seq, H_A, DK_A, DV_A, z.shape[0], row0, prev_out)


def _gla_scan(y, z, wg, bg, init_states, row0, batch, seq, low_col_block, prev_out=None):
    v0 = 2 * H_B * DK_B // DV_B
    arrays = [
        (y, DK_B, 0, (0, 0), True),
        (y, DK_B, 0, (H_B, H_B), True),
        (y, DV_B, 0, (v0, v0), True),
        (z, LANE, row0, (low_col_block, low_col_block), False),
    ]
    consts = [(wg[0], (LANE, DK_B)), (wg[1], (LANE, DK_B)), (bg[0:1], (1, DK_B)), (bg[1:2], (1, DK_B))]
    return _scan("gla", arrays, consts, init_states, batch, seq, H_B, DK_B, DV_B, z.shape[0], row0, prev_out)


def _conv_kernel(rows, width, x_ref, w_ref, o_ref):
    seq = rows * width
    x = x_ref[...]
    assert width & (width - 1) == 0
    l = lax.broadcasted_iota(jnp.int32, (seq, 1), 0)
    r, c = l >> (width.bit_length() - 1), l & (width - 1)
    acc = jnp.zeros_like(x)
    for di in range(3):
        if rows == 1 and di != 1:
            continue
        for dj in range(3):
            k = (di - 1) * width + (dj - 1)
            xs = x if k == 0 else pltpu.roll(x, (-k) % seq, 0)
            ok = (r + (di - 1) >= 0) & (r + (di - 1) < rows) & (c + (dj - 1) >= 0) & (c + (dj - 1) < width)
            acc = acc + jnp.where(ok, xs, 0.0) * w_ref[3 * di + dj:3 * di + dj + 1, :]
    o_ref[...] = _silu(acc)


def _conv_silu(z, conv_w9, row0, batch, rows, width, tc=256):
    seq = rows * width
    sb0 = row0 // seq
    assert row0 % seq == 0
    return pl.pallas_call(
        functools.partial(_conv_kernel, rows, width),
        grid=(batch, QKV_B // tc),
        in_specs=[
            pl.BlockSpec((seq, tc), lambda b, j: (sb0 + b, j)),
            pl.BlockSpec((9, tc), lambda b, j: (0, j)),
        ],
        out_specs=pl.BlockSpec((seq, tc), lambda b, j: (b, j)),
        out_shape=jax.ShapeDtypeStruct((batch * seq, QKV_B), F32),
        compiler_params=_cparams(("parallel", "parallel")),
        name="conv_silu",
    )(z, conv_w9)


def _outproj_kernel(heads, dvh, of_ref, ob_ref, og_ref, gw_ref, w_ref, x_ref, g_ref, o_ref):
    o = of_ref[...] + ob_ref[...]
    og = og_ref[...]
    gw = gw_ref[...]
    parts = []
    for h in range(heads):
        oh = o[:, h * dvh:(h + 1) * dvh]
        ms = jnp.mean(oh * oh, axis=-1, keepdims=True)
        parts.append(oh * lax.rsqrt(ms + EPS) * gw)
    y = jnp.concatenate(parts, axis=1) * _silu(og)
    o_ref[...] = x_ref[...] + g_ref[0] * _dot(y.astype(BF16), w_ref[...])


def _outproj(o_f, o_b, z, og_col_block, gnorm_w, w_out_bf16, x, mod3, stream, layer, heads, dvh):
    t, d = x.shape
    tm = 256
    st = _Stream(stream.n_prompt_blocks * stream.tm, stream.blocks_per_sample * stream.tm, tm)
    return pl.pallas_call(
        functools.partial(_outproj_kernel, heads, dvh),
        grid=(t // tm,),
        in_specs=[
            pl.BlockSpec((tm, d), lambda i: (i, 0)),
            pl.BlockSpec((tm, d), lambda i: (i, 0)),
            pl.BlockSpec((tm, d), lambda i: (i, og_col_block)),
            pl.BlockSpec((1, dvh), lambda i: (0, 0)),
            pl.BlockSpec((d, d), lambda i: (0, 0)),
            pl.BlockSpec((tm, d), lambda i: (i, 0)),
            pl.BlockSpec((1, 1, d), st.mod_index(layer, 2)),
        ],
        out_specs=pl.BlockSpec((tm, d), lambda i: (i, 0)),
        out_shape=jax.ShapeDtypeStruct((t, d), F32),
        compiler_params=_cparams(("parallel",)),
        name="outproj",
    )(o_f, o_b, z, gnorm_w.reshape(1, dvh), w_out_bf16, x, mod3)


def _top_rows(scores, k):
    n, tm = scores[0].shape
    rid = lax.broadcasted_iota(jnp.int32, (n, tm), 0).astype(F32)
    kid = lax.broadcasted_iota(jnp.int32, (k, tm), 0)

    def body(j, carry):
        out = []
        for s, vals, ids in carry:
            m = jnp.max(s, axis=0, keepdims=True)
            pos = jnp.min(jnp.where(s == m, rid, float(n)), axis=0, keepdims=True)
            vals = jnp.where(kid == j, m, vals)
            ids = jnp.where(kid == j, pos, ids)
            out.append((jnp.where(rid == pos, NEG_INF, s), vals, ids))
        return tuple(out)

    init = tuple((s, jnp.zeros((k, tm), F32), jnp.zeros((k, tm), F32)) for s in scores)
    res = lax.fori_loop(0, k, body, init)
    return [r[1] for r in res], [r[2] for r in res]


_PAIR_COUNTS = [P_TOPK // (a + 1) for a in range(P_TOPK)]
_N_PAIRS = sum(_PAIR_COUNTS)
_N_PAIR_ROWS = -(-_N_PAIRS // 8) * 8


def _route_kernel(q_ref, keys_ref, idx_ref, gate_ref, cand_ref, cid_ref):
    tm = q_ref.shape[0]
    scores = [_dot_nt(keys_ref[0, p], q_ref[:, p * P_HALF:(p + 1) * P_HALF]) for p in range(2)]
    vals, ids = _top_rows(scores, P_TOPK)
    cand_ref[...] = jnp.full(cand_ref.shape, NEG_INF, F32)
    cid_ref[...] = jnp.zeros(cid_ref.shape, F32)
    r0 = 0
    for a, nb in enumerate(_PAIR_COUNTS):
        cand_ref[r0:r0 + nb, :] = vals[0][a:a + 1] + vals[1][0:nb]
        cid_ref[r0:r0 + nb, :] = ids[0][a:a + 1] * float(N_KEYS) + ids[1][0:nb]
        r0 += nb
    cand = cand_ref[...]
    cid = cid_ref[...]
    n = cand.shape[0]
    rid = lax.broadcasted_iota(jnp.int32, (n, tm), 0).astype(F32)
    kid = lax.broadcasted_iota(jnp.int32, (P_TOPK, tm), 0)

    def body(j, carry):
        cand, top, idx = carry
        m = jnp.max(cand, axis=0, keepdims=True)
        pos = jnp.min(jnp.where(cand == m, rid, float(n)), axis=0, keepdims=True)
        sel = rid == pos
        e = jnp.sum(jnp.where(sel, cid, 0.0), axis=0, keepdims=True)
        top = jnp.where(kid == j, m, top)
        idx = jnp.where(kid == j, e, idx)
        return jnp.where(sel, NEG_INF, cand), top, idx

    _, top, idx = lax.fori_loop(
        0, P_TOPK, body, (cand, jnp.zeros((P_TOPK, tm), F32), jnp.zeros((P_TOPK, tm), F32)))
    ex = jnp.exp(top - top[0:1])
    gate_ref[...] = ex / jnp.sum(ex, axis=0, keepdims=True)
    idx_ref[...] = idx.astype(jnp.int32)


def _route(q, keys):
    t = q.shape[0]
    tm = 256
    return pl.pallas_call(
        _route_kernel,
        grid=(t // tm, P_HEADS),
        in_specs=[
            pl.BlockSpec((tm, 2 * P_HALF), lambda i, h: (i, h)),
            pl.BlockSpec((1, 2, N_KEYS, P_HALF), lambda i, h: (h, 0, 0, 0)),
        ],
        out_specs=[
            pl.BlockSpec((P_TOPK, tm), lambda i, h: (h, i)),
            pl.BlockSpec((P_TOPK, tm), lambda i, h: (h, i)),
        ],
        out_shape=[
            jax.ShapeDtypeStruct((N_SEL, t), jnp.int32),
            jax.ShapeDtypeStruct((N_SEL, t), F32),
        ],
        scratch_shapes=[pltpu.VMEM((_N_PAIR_ROWS, tm), F32), pltpu.VMEM((_N_PAIR_ROWS, tm), F32)],
        compiler_params=_cparams(("parallel", "parallel")),
        name="peer_route",
    )(q, keys)


PEER_TB = 128
PEER_NBUF = 8
SUBLANES = 8
D_TILES = D_MODEL // LANE


def _sublane_sums(p):
    s = lax.broadcasted_iota(jnp.int32, p.shape[2:], 0)
    sel = s < 4
    a, b = p[:, 0:4], p[:, 4:8]
    f = jnp.where(sel, a, b) + pltpu.roll(jnp.where(sel, b, a), 4, 2)
    sel = (s & 2) == 0
    a, b = f[:, 0:2], f[:, 2:4]
    g = jnp.where(sel, a + pltpu.roll(a, 6, 2), b + pltpu.roll(b, 2, 2))
    sel = (s & 1) == 0
    a, b = g[:, 0], g[:, 1]
    return jnp.where(sel, a + pltpu.roll(a, 7, 1), b + pltpu.roll(b, 1, 1))


def _peer_kernel(idx_ref, idx_next_ref, h_ref, gate_ref, x_ref, g2_ref, uv_hbm, o_ref, *scratch):
    bufs, cb_ref, sem = scratch[:PEER_NBUF], scratch[PEER_NBUF], scratch[PEER_NBUF + 1]
    step, nsteps = pl.program_id(0), pl.num_programs(0)

    def start_token(ids_ref, t, slot):
        for e in range(N_SEL):
            pltpu.make_async_copy(uv_hbm.at[ids_ref[t, e]], bufs[slot].at[e], sem.at[slot]).start(priority=e % 2)

    def wait_token(slot):
        pltpu.make_async_copy(uv_hbm.at[pl.ds(0, N_SEL)], bufs[slot], sem.at[slot]).wait()

    eye = lax.broadcasted_iota(jnp.int32, (N_SEL, N_SEL), 0) == lax.broadcasted_iota(jnp.int32, (N_SEL, N_SEL), 1)
    g2 = g2_ref[0]

    def compute(t, slot):
        buf = bufs[slot]
        prod = buf[:, 0:SUBLANES, :] * h_ref[t]
        rows = _sublane_sums(prod.reshape(N_SEL // 8, 8, SUBLANES, LANE))
        s = jnp.sum(rows.reshape(N_SEL, LANE), axis=1, keepdims=True)
        gcol = jnp.sum(jnp.where(eye, gate_ref[pl.ds(t, 1), :], 0.0), axis=1, keepdims=True)
        cb_ref[...] = jnp.broadcast_to(gcol * _gelu(s), (N_SEL, LANE))
        accs = [jnp.zeros((SUBLANES, LANE), F32) for _ in range(4)]
        for e in range(N_SEL):
            accs[e % 4] = accs[e % 4] + cb_ref[pl.ds(e, 1), :] * buf[e, SUBLANES:2 * SUBLANES, :]
        o_ref[t] = x_ref[t] + g2 * ((accs[0] + accs[1]) + (accs[2] + accs[3]))

    ahead = PEER_NBUF - 1

    @pl.when(step == 0)
    def _():
        for t in range(ahead):
            start_token(idx_ref, t, t)

    def group(t0, last):
        for k in range(PEER_NBUF):
            wait_token(k)
            nxt, slot = t0 + k + ahead, (k + ahead) % PEER_NBUF
            if not last:
                start_token(idx_ref, nxt, slot)
            elif k + ahead < PEER_NBUF:
                start_token(idx_ref, PEER_TB - PEER_NBUF + k + ahead, slot)
            else:
                @pl.when(step + 1 < nsteps)
                def _():
                    start_token(idx_next_ref, k + ahead - PEER_NBUF, slot)
            compute(t0 + k, k)

    def body(i, _):
        group(PEER_NBUF * i, False)
        return 0

    lax.fori_loop(0, PEER_TB // PEER_NBUF - 1, body, 0)
    group(PEER_TB - PEER_NBUF, True)


def _peer(idx_t, gate_t, h, x, mod3, stream, layer, uv3):
    t, d = x.shape
    tb = PEER_TB
    st = _Stream(stream.n_prompt_blocks * stream.tm, stream.blocks_per_sample * stream.tm, tb)
    tok_spec = pl.BlockSpec((tb, D_TILES, LANE), lambda i: (i, 0, 0))
    last_block = t // tb - 1
    out = pl.pallas_call(
        _peer_kernel,
        grid=(t // tb,),
        in_specs=[
            pl.BlockSpec((tb, N_SEL), lambda i: (i, 0), memory_space=pltpu.SMEM),
            pl.BlockSpec((tb, N_SEL), lambda i: (jnp.minimum(i + 1, last_block), 0), memory_space=pltpu.SMEM),
            tok_spec,
            pl.BlockSpec((tb, N_SEL), lambda i: (i, 0)),
            tok_spec,
            pl.BlockSpec((1, D_TILES, LANE), st.mod_index(layer, 5)),
            pl.BlockSpec(memory_space=pl.ANY),
        ],
        out_specs=tok_spec,
        out_shape=jax.ShapeDtypeStruct((t, D_TILES, LANE), F32),
        scratch_shapes=[pltpu.VMEM((N_SEL, 2 * SUBLANES, LANE), F32) for _ in range(PEER_NBUF)] + [
            pltpu.VMEM((N_SEL, LANE), F32),
            pltpu.SemaphoreType.DMA((PEER_NBUF,)),
        ],
        compiler_params=_cparams(("arbitrary",)),
        name="peer_experts",
    )(idx_t, idx_t, h.reshape(t, D_TILES, LANE), gate_t, x.reshape(t, D_TILES, LANE),
      mod3.reshape(-1, D_TILES, LANE), uv3)
    return out.reshape(t, d)


def _rms_kernel(x_ref, w_ref, o_ref):
    x = x_ref[...]
    ms = jnp.mean(x * x, axis=-1, keepdims=True)
    o_ref[...] = x * lax.rsqrt(ms + EPS) * w_ref[...]


def _rms_norm(x, w, row0, rows):
    d = x.shape[1]
    tm = 512
    assert row0 % tm == 0 and rows % tm == 0
    return pl.pallas_call(
        _rms_kernel,
        grid=(rows // tm,),
        in_specs=[pl.BlockSpec((tm, d), lambda i: (row0 // tm + i, 0)), pl.BlockSpec((1, d), lambda i: (0, 0))],
        out_specs=pl.BlockSpec((tm, d), lambda i: (i, 0)),
        out_shape=jax.ShapeDtypeStruct((rows, d), F32),
        compiler_params=_cparams(("parallel",)),
        name="final_norm",
    )(x, w.reshape(1, d))


def kernel(x_prompt, x_sample, state_hgrn, state_gla, c, c_ctx, ada_w, ada_b, norm1_w, norm2_w, final_norm_w, hgrn_w_in, hgrn_lb, hgrn_gnorm_w, hgrn_w_out, gla_w_in, gla_conv_w, gla_w_gate, gla_b_gate, gla_gnorm_w, gla_w_out, peer_w_q, peer_keys, peer_u, peer_v):
    bp, lp, d = x_prompt.shape
    bs, ls, _ = x_sample.shape
    t_p, t_s = bp * lp, bs * ls
    rows_s = ls // GRID_W
    stream = _Stream(t_p, ls, ROW_TILE)

    x = jnp.concatenate([x_prompt.reshape(t_p, d), x_sample.reshape(t_s, d)], axis=0)

    cvec = jnp.concatenate([c_ctx[None, :], c, jnp.zeros((MOD_ROWS - 1 - bs, d), F32)], axis=0)
    mod3 = _adaln(cvec, ada_w, ada_b).reshape(DEPTH * MOD_ROWS * 6, 1, d)

    p = jax.nn.softmax(hgrn_lb.astype(F32), axis=0)
    lower_bounds = jnp.cumsum(p, axis=0) - p[0:1]

    gla_n = gla_w_in.shape[2]
    gla_np = -(-gla_n // LANE) * LANE
    low_col_block = (QKV_B + D_MODEL) // LANE
    n_b = gla_w_in.shape[0]
    wg_pad = jnp.zeros((n_b, 2, LANE, H_B * DK_B), F32)
    wg_pad = wg_pad.at[:, 0, 0:GLA_RANK].set(gla_w_gate[:, 0]).at[:, 1, GLA_RANK:2 * GLA_RANK].set(gla_w_gate[:, 1])

    o_zero = jnp.zeros((t_p + t_s, d), F32)

    new_hgrn, new_gla = [], []
    for l in range(DEPTH):
        j = l // 2
        if l % 2 == 0:
            z = _norm_mod_matmul(x, norm1_w[l], mod3, stream, l, 0, 1, hgrn_w_in[j].astype(BF16), 1024)
            lb = lower_bounds[j]
            of_p, ob_p, sf, sb = _hgrn_scan(z, lb, None, 0, bp, lp, prev_out=(o_zero, o_zero))
            init = (jnp.swapaxes(state_hgrn[:, j, 0], -1, -2), jnp.swapaxes(state_hgrn[:, j, 1], -1, -2))
            o_f, o_b, _, _ = _hgrn_scan(z, lb, init, t_p, bs, ls, prev_out=(of_p, ob_p))
            new_hgrn.append(jnp.stack([jnp.swapaxes(sf, -1, -2), jnp.swapaxes(sb, -1, -2)], axis=1))
            heads, dvh, og_blk = H_A, DV_A, 4
            gnw, w_out = hgrn_gnorm_w[j], hgrn_w_out[j]
        else:
            w_in = jnp.pad(gla_w_in[j], ((0, 0), (0, gla_np - gla_n))).astype(BF16)
            z = _norm_mod_matmul(x, norm1_w[l], mod3, stream, l, 0, 1, w_in, gla_np // 5 if gla_np % 5 == 0 else LANE)
            cw = gla_conv_w[j].reshape(9, QKV_B)
            y_p = _conv_silu(z, cw, 0, bp, 1, lp)
            y_s = _conv_silu(z, cw, t_p, bs, rows_s, GRID_W)
            wg, bg = wg_pad[j], gla_b_gate[j]
            of_p, ob_p, sf, sb = _gla_scan(y_p, z, wg, bg, None, 0, bp, lp, low_col_block, prev_out=(o_zero, o_zero))
            init = (jnp.swapaxes(state_gla[:, j, 0], -1, -2), jnp.swapaxes(state_gla[:, j, 1], -1, -2))
            o_f, o_b, _, _ = _gla_scan(y_s, z, wg, bg, init, t_p, bs, ls, low_col_block, prev_out=(of_p, ob_p))
            new_gla.append(jnp.stack([jnp.swapaxes(sf, -1, -2), jnp.swapaxes(sb, -1, -2)], axis=1))
            heads, dvh, og_blk = H_B, DV_B, 2
            gnw, w_out = gla_gnorm_w[j], gla_w_out[j]
        x = _outproj(o_f, o_b, z, og_blk, gnw, w_out.astype(BF16), x, mod3, stream, l, heads, dvh)

        q, h = _norm_mod_matmul(x, norm2_w[l], mod3, stream, l, 3, 4, peer_w_q[l].astype(BF16), 1024, emit_h=True)
        idx, gates = _route(q, peer_keys[l])
        n_exp = peer_u.shape[1]
        uv3 = jnp.concatenate([peer_u[l].reshape(n_exp, D_TILES, LANE), peer_v[l].reshape(n_exp, D_TILES, LANE)], axis=1)
        x = _peer(idx.T, gates.T, h, x, mod3, stream, l, uv3)

    y_prompt = _rms_norm(x, final_norm_w, 0, t_p).reshape(bp, lp, d)
    y_sample = _rms_norm(x, final_norm_w, t_p, t_s).reshape(bs, ls, d)
    return (y_prompt, y_sample, jnp.stack(new_hgrn, axis=1), jnp.stack(new_gla, axis=1))
```

```python
import functools

import jax
import jax.numpy as jnp
from jax import lax
from jax.experimental import pallas as pl
from jax.experimental.pallas import tpu as pltpu

F32 = jnp.float32
BF16 = jnp.bfloat16

D_MODEL = 1024
DEPTH = 4
GRID_W = 64
EPS = 1e-6
F_MIN = 1e-20
H_A, DK_A, DV_A = 8, 128, 128
H_B, DK_B, DV_B = 4, 128, 256
QKV_B = 2048
GLA_RANK = 16
GATE_NORM = 16.0
P_HEADS, N_KEYS, P_TOPK, P_HALF = 8, 128, 16, 128
N_SEL = P_HEADS * P_TOPK

V7X_VMEM_LIMIT = 48 * 1024 * 1024
LANE = 128
ROW_TILE = 1024
CHUNK = 64
SUB = 16
MOD_ROWS = 16
NEG_INF = float("-inf")


def _cparams(sem):
    return pltpu.CompilerParams(dimension_semantics=sem, vmem_limit_bytes=V7X_VMEM_LIMIT)


def _silu(x):
    return x * jax.nn.sigmoid(x)


def _gelu(x):
    return 0.5 * x * (1.0 + lax.erf(x * (2.0 ** -0.5)))


def _dot(a, b):
    return jnp.dot(a, b, preferred_element_type=F32)


def _dot_nt(a, b):
    return lax.dot_general(a, b, (((1,), (1,)), ((), ())), preferred_element_type=F32)


def _dot_tn(a, b):
    return lax.dot_general(a, b, (((0,), (0,)), ((), ())), preferred_element_type=F32)


def _dot_01(m01, x):
    x1 = x.astype(BF16)
    r1 = x - x1.astype(F32)
    x2 = r1.astype(BF16)
    x3 = (r1 - x2.astype(F32)).astype(BF16)
    return _dot(m01, x1) + _dot(m01, x2) + _dot(m01, x3)


def _adaln_kernel(c_ref, w_ref, b_ref, o_ref):
    o_ref[0] = _dot(_silu(c_ref[...]), w_ref[0]) + b_ref[0]


def _adaln(cvec, ada_w, ada_b):
    tn = 1536
    n = 6 * D_MODEL
    return pl.pallas_call(
        _adaln_kernel,
        grid=(DEPTH, n // tn),
        in_specs=[
            pl.BlockSpec((MOD_ROWS, D_MODEL), lambda l, j: (0, 0)),
            pl.BlockSpec((1, D_MODEL, tn), lambda l, j: (l, 0, j)),
            pl.BlockSpec((1, 1, tn), lambda l, j: (l, 0, j)),
        ],
        out_specs=pl.BlockSpec((1, MOD_ROWS, tn), lambda l, j: (l, 0, j)),
        out_shape=jax.ShapeDtypeStruct((DEPTH, MOD_ROWS, n), F32),
        compiler_params=_cparams(("parallel", "parallel")),
        name="adaln",
    )(cvec, ada_w, ada_b.reshape(DEPTH, 1, n))


class _Stream:
    def __init__(self, t_prompt, seq_sample, tm):
        assert t_prompt % tm == 0 and seq_sample % tm == 0
        self.tm = tm
        self.n_prompt_blocks = t_prompt // tm
        self.blocks_per_sample = seq_sample // tm

    def mod_index(self, layer, k):
        npb, bps = self.n_prompt_blocks, self.blocks_per_sample

        def f(i, *_):
            row = jnp.where(i < npb, 0, 1 + (i - npb) // bps)
            return ((layer * MOD_ROWS + row) * 6 + k, 0, 0)

        return f


def _norm_mod(x, nw, shift, scale):
    ms = jnp.mean(x * x, axis=-1, keepdims=True)
    return (x * lax.rsqrt(ms + EPS) * nw) * (1.0 + scale) + shift


def _nmm_kernel(emit_h, x_ref, nw_ref, sh_ref, sc_ref, w_ref, *rest):
    if emit_h:
        o_ref, h_ref, hs_ref = rest
    else:
        o_ref, hs_ref = rest

    @pl.when(pl.program_id(1) == 0)
    def _():
        h = _norm_mod(x_ref[...], nw_ref[...], sh_ref[0], sc_ref[0])
        hs_ref[...] = h.astype(BF16)
        if emit_h:
            h_ref[...] = h

    o_ref[...] = _dot(hs_ref[...], w_ref[...])


def _norm_mod_matmul(x, nw, mod3, stream, layer, k_shift, k_scale, w_bf16, tn, emit_h=False):
    t, d = x.shape
    n = w_bf16.shape[1]
    tm = stream.tm
    assert n % tn == 0
    out_shape = [jax.ShapeDtypeStruct((t, n), F32)]
    out_specs = [pl.BlockSpec((tm, tn), lambda i, j: (i, j))]
    if emit_h:
        out_shape.append(jax.ShapeDtypeStruct((t, d), F32))
        out_specs.append(pl.BlockSpec((tm, d), lambda i, j: (i, 0)))
    res = pl.pallas_call(
        functools.partial(_nmm_kernel, emit_h),
        grid=(t // tm, n // tn),
        in_specs=[
            pl.BlockSpec((tm, d), lambda i, j: (i, 0)),
            pl.BlockSpec((1, d), lambda i, j: (0, 0)),
            pl.BlockSpec((1, 1, d), stream.mod_index(layer, k_shift)),
            pl.BlockSpec((1, 1, d), stream.mod_index(layer, k_scale)),
            pl.BlockSpec((d, tn), lambda i, j: (0, j)),
        ],
        out_specs=out_specs,
        out_shape=out_shape,
        scratch_shapes=[pltpu.VMEM((tm, d), BF16)],
        compiler_params=_cparams(("parallel", "arbitrary")),
        name="norm_mod_matmul",
    )(x, nw.reshape(1, d), mod3, mod3, w_bf16)
    return res if emit_h else res[0]


def _scan_direction(q, k, v, g, st_ref, sidx, rev):
    c = CHUNK
    row = lax.broadcasted_iota(jnp.int32, (c, c), 0)
    col = lax.broadcasted_iota(jnp.int32, (c, c), 1)
    cmat = jnp.where((col >= row) if rev else (col <= row), 1.0, 0.0).astype(BF16)
    cs = _dot_01(cmat, g)
    st = st_ref[sidx]
    o = _dot_nt(q * jnp.exp(cs), st)
    edge = cs[0:1] if rev else cs[c - 1:c]
    st_ref[sidx] = st * jnp.exp(edge) + _dot_tn(v, k * jnp.exp(edge - cs))

    srow = lax.broadcasted_iota(jnp.int32, (SUB, c), 0)
    scol = lax.broadcasted_iota(jnp.int32, (SUB, c), 1)
    att_rows = []
    for blk in range(c // SUB):
        r0 = blk * SUB
        q_i, k_i, cs_i = q[r0:r0 + SUB], k[r0:r0 + SUB], cs[r0:r0 + SUB]
        has_off = (blk < c // SUB - 1) if rev else (blk > 0)
        if has_off:
            ref = cs[r0 + SUB:r0 + SUB + 1] if rev else cs[r0 - 1:r0]
            w_i = cs_i - ref
            kt = k * jnp.exp(jnp.minimum(ref - cs, 0.0))
            off = _dot_nt(q_i * jnp.exp(w_i), kt)
            off_mask = (scol >= r0 + SUB) if rev else (scol < r0)
            a_i = jnp.where(off_mask, off, 0.0)
        else:
            w_i = cs_i
            a_i = jnp.zeros((SUB, c), F32)
        for s in range(SUB):
            e = jnp.exp(jnp.minimum(w_i - w_i[s:s + 1], 0.0))
            colv = jnp.sum(q_i * e * k_i[s:s + 1], axis=1, keepdims=True)
            keep = (srow <= s) if rev else (srow >= s)
            a_i = jnp.where((scol == r0 + s) & keep, colv, a_i)
        att_rows.append(a_i)
    att = jnp.concatenate(att_rows, axis=0)
    return o + _dot(att, v)


def _log_sigmoid(x):
    return jnp.minimum(x, 0.0) - jnp.log(1.0 + jnp.exp(-jnp.abs(x)))


SCAN_HPB = 8


def _scan_kernel(mode, zero_init, n_aliased, hpb, dk, dv, *refs):
    if mode == "hgrn":
        (qf, vf, ff, qb, vb, fb, lbf, lbb), rest = refs[:8], refs[8:]
    else:
        (qf, kf, vf, lowf, qb, kb, vb, lowb, wgf, wgb, bgf, bgb), rest = refs[:12], refs[12:]
    if not zero_init:
        (s0f, s0b), rest = rest[:2], rest[2:]
    rest = rest[n_aliased:]
    of_ref, ob_ref, sf_ref, sb_ref, st_ref = rest
    ci = pl.program_id(2)

    @pl.when(ci == 0)
    def _():
        if zero_init:
            st_ref[...] = jnp.zeros_like(st_ref)
        else:
            for j in range(hpb):
                st_ref[j] = s0f[0, j]
                st_ref[hpb + j] = s0b[0, j]

    def prep(ks, q_ref, k_ref, f_ref, lb_ref, low_ref, wg_ref, bg_ref):
        if mode == "hgrn":
            lb = lb_ref[:, ks]
            fp = f_ref[:, ks]
            f = lb + (1.0 - lb) * jax.nn.sigmoid(fp)
            g = jnp.log(jnp.maximum(f, F_MIN))
            k = (1.0 - lb) * jax.nn.sigmoid(-fp)
            q = _silu(q_ref[:, ks])
        else:
            gk = _dot(low_ref[...], wg_ref[:, ks]) + bg_ref[:, ks]
            g = _log_sigmoid(gk) * (1.0 / GATE_NORM)
            k = k_ref[:, ks]
            q = q_ref[:, ks] * (DK_B ** -0.5)
        return q, k, g

    for j in range(hpb):
        ks, vs = slice(j * dk, (j + 1) * dk), slice(j * dv, (j + 1) * dv)
        if mode == "hgrn":
            q, k, g = prep(ks, qf, None, ff, lbf, None, None, None)
            of_ref[:, vs] = _scan_direction(q, k, vf[:, vs], g, st_ref, j, False)
            q, k, g = prep(ks, qb, None, fb, lbb, None, None, None)
            ob_ref[:, vs] = _scan_direction(q, k, vb[:, vs], g, st_ref, hpb + j, True)
        else:
            q, k, g = prep(ks, qf, kf, None, None, lowf, wgf, bgf)
            of_ref[:, vs] = _scan_direction(q, k, vf[:, vs], g, st_ref, j, False)
            q, k, g = prep(ks, qb, kb, None, None, lowb, wgb, bgb)
            ob_ref[:, vs] = _scan_direction(q, k, vb[:, vs], g, st_ref, hpb + j, True)

    @pl.when(ci == pl.num_programs(2) - 1)
    def _():
        for j in range(hpb):
            sf_ref[0, j] = st_ref[j]
            sb_ref[0, j] = st_ref[hpb + j]


def _scan(mode, arrays, consts, init_states, batch, seq, heads, dk, dv, out_rows, out_row0, prev_out):
    n = seq // CHUNK
    hpb = min(SCAN_HPB, heads)
    assert heads % hpb == 0 and out_row0 % CHUNK == 0
    zero_init = init_states is None
    orb0 = out_row0 // CHUNK

    def chunk_of(c, dirn):
        return c if dirn == 0 else n - 1 - c

    in_specs, operands = [], []
    for dirn in (0, 1):
        for arr, width, row0, cols, per_head in arrays:
            assert row0 % CHUNK == 0 and (not per_head or cols[dirn] % hpb == 0)

            def imap(b, hg, c, dirn=dirn, rb0=row0 // CHUNK, col0=cols[dirn], per_head=per_head):
                return (rb0 + b * n + chunk_of(c, dirn), (col0 // hpb + hg) if per_head else col0)

            in_specs.append(pl.BlockSpec((CHUNK, width * hpb if per_head else width), imap))
            operands.append(arr)
    for arr, shape in consts:
        in_specs.append(pl.BlockSpec((shape[0], shape[1] * hpb), lambda b, hg, c: (0, hg)))
        operands.append(arr)
    s_spec = pl.BlockSpec((1, hpb, dv, dk), lambda b, hg, c: (b, hg, 0, 0))
    if not zero_init:
        for s0 in init_states:
            in_specs.append(s_spec)
            operands.append(s0)
    aliases = {}
    if prev_out is not None:
        for k, arr in enumerate(prev_out):
            aliases[len(operands)] = k
            in_specs.append(pl.BlockSpec(memory_space=pl.ANY))
            operands.append(arr)
    o_specs = [pl.BlockSpec((CHUNK, dv * hpb), lambda b, hg, c, dirn=dirn: (orb0 + b * n + chunk_of(c, dirn), hg))
               for dirn in (0, 1)]
    return pl.pallas_call(
        functools.partial(_scan_kernel, mode, zero_init, len(aliases), hpb, dk, dv),
        grid=(batch, heads // hpb, n),
        in_specs=in_specs,
        out_specs=o_specs + [s_spec, s_spec],
        out_shape=[
            jax.ShapeDtypeStruct((out_rows, heads * dv), F32),
            jax.ShapeDtypeStruct((out_rows, heads * dv), F32),
            jax.ShapeDtypeStruct((batch, heads, dv, dk), F32),
            jax.ShapeDtypeStruct((batch, heads, dv, dk), F32),
        ],
        scratch_shapes=[pltpu.VMEM((2 * hpb, dv, dk), F32)],
        input_output_aliases=aliases,
        compiler_params=_cparams(("parallel", "parallel", "arbitrary")),
        name="scan_" + mode,
    )(*operands)


def _hgrn_scan(z, lb, init_states, row0, batch, seq, prev_out=None):
    nh = H_A
    arrays = [
        (z, DK_A, row0, (0, 0), True),
        (z, DV_A, row0, (nh, nh), True),
        (z, DK_A, row0, (2 * nh, 3 * nh), True),
    ]
    consts = [(lb[0:1], (1, DK_A)), (lb[1:2], (1, DK_A))]
    return _scan("hgrn", arrays, consts, init_states, batch, seq, H_A, DK_A, DV_A, z.shape[0], row0, prev_out)


def _gla_scan(y, z, wg, bg, init_states, row0, batch, seq, low_col_block, prev_out=None):
    v0 = 2 * H_B * DK_B // DV_B
    arrays = [
        (y, DK_B, 0, (0, 0), True),
        (y, DK_B, 0, (H_B, H_B), True),
        (y, DV_B, 0, (v0, v0), True),
        (z, LANE, row0, (low_col_block, low_col_block), False),
    ]
    consts = [(wg[0], (LANE, DK_B)), (wg[1], (LANE, DK_B)), (bg[0:1], (1, DK_B)), (bg[1:2], (1, DK_B))]
    return _scan("gla", arrays, consts, init_states, batch, seq, H_B, DK_B, DV_B, z.shape[0], row0, prev_out)


def _conv_kernel(rows, width, x_ref, w_ref, o_ref):
    seq = rows * width
    x = x_ref[...]
    assert width & (width - 1) == 0
    l = lax.broadcasted_iota(jnp.int32, (seq, 1), 0)
    r, c = l >> (width.bit_length() - 1), l & (width - 1)
    acc = jnp.zeros_like(x)
    for di in range(3):
        if rows == 1 and di != 1:
            continue
        for dj in range(3):
            k = (di - 1) * width + (dj - 1)
            xs = x if k == 0 else pltpu.roll(x, (-k) % seq, 0)
            ok = (r + (di - 1) >= 0) & (r + (di - 1) < rows) & (c + (dj - 1) >= 0) & (c + (dj - 1) < width)
            acc = acc + jnp.where(ok, xs, 0.0) * w_ref[3 * di + dj:3 * di + dj + 1, :]
    o_ref[...] = _silu(acc)


def _conv_silu(z, conv_w9, row0, batch, rows, width, tc=256):
    seq = rows * width
    sb0 = row0 // seq
    assert row0 % seq == 0
    return pl.pallas_call(
        functools.partial(_conv_kernel, rows, width),
        grid=(batch, QKV_B // tc),
        in_specs=[
            pl.BlockSpec((seq, tc), lambda b, j: (sb0 + b, j)),
            pl.BlockSpec((9, tc), lambda b, j: (0, j)),
        ],
        out_specs=pl.BlockSpec((seq, tc), lambda b, j: (b, j)),
        out_shape=jax.ShapeDtypeStruct((batch * seq, QKV_B), F32),
        compiler_params=_cparams(("parallel", "parallel")),
        name="conv_silu",
    )(z, conv_w9)


def _outproj_kernel(heads, dvh, of_ref, ob_ref, og_ref, gw_ref, w_ref, x_ref, g_ref, o_ref):
    o = of_ref[...] + ob_ref[...]
    og = og_ref[...]
    gw = gw_ref[...]
    parts = []
    for h in range(heads):
        oh = o[:, h * dvh:(h + 1) * dvh]
        ms = jnp.mean(oh * oh, axis=-1, keepdims=True)
        parts.append(oh * lax.rsqrt(ms + EPS) * gw)
    y = jnp.concatenate(parts, axis=1) * _silu(og)
    o_ref[...] = x_ref[...] + g_ref[0] * _dot(y.astype(BF16), w_ref[...])


def _outproj(o_f, o_b, z, og_col_block, gnorm_w, w_out_bf16, x, mod3, stream, layer, heads, dvh):
    t, d = x.shape
    tm = 256
    st = _Stream(stream.n_prompt_blocks * stream.tm, stream.blocks_per_sample * stream.tm, tm)
    return pl.pallas_call(
        functools.partial(_outproj_kernel, heads, dvh),
        grid=(t // tm,),
        in_specs=[
            pl.BlockSpec((tm, d), lambda i: (i, 0)),
            pl.BlockSpec((tm, d), lambda i: (i, 0)),
            pl.BlockSpec((tm, d), lambda i: (i, og_col_block)),
            pl.BlockSpec((1, dvh), lambda i: (0, 0)),
            pl.BlockSpec((d, d), lambda i: (0, 0)),
            pl.BlockSpec((tm, d), lambda i: (i, 0)),
            pl.BlockSpec((1, 1, d), st.mod_index(layer, 2)),
        ],
        out_specs=pl.BlockSpec((tm, d), lambda i: (i, 0)),
        out_shape=jax.ShapeDtypeStruct((t, d), F32),
        compiler_params=_cparams(("parallel",)),
        name="outproj",
    )(o_f, o_b, z, gnorm_w.reshape(1, dvh), w_out_bf16, x, mod3)


def _top_rows(scores, k):
    n, tm = scores[0].shape
    rid = lax.broadcasted_iota(jnp.int32, (n, tm), 0).astype(F32)
    kid = lax.broadcasted_iota(jnp.int32, (k, tm), 0)

    def body(j, carry):
        out = []
        for s, vals, ids in carry:
            m = jnp.max(s, axis=0, keepdims=True)
            pos = jnp.min(jnp.where(s == m, rid, float(n)), axis=0, keepdims=True)
            vals = jnp.where(kid == j, m, vals)
            ids = jnp.where(kid == j, pos, ids)
            out.append((jnp.where(rid == pos, NEG_INF, s), vals, ids))
        return tuple(out)

    init = tuple((s, jnp.zeros((k, tm), F32), jnp.zeros((k, tm), F32)) for s in scores)
    res = lax.fori_loop(0, k, body, init)
    return [r[1] for r in res], [r[2] for r in res]


_PAIR_COUNTS = [P_TOPK // (a + 1) for a in range(P_TOPK)]
_N_PAIRS = sum(_PAIR_COUNTS)
_N_PAIR_ROWS = -(-_N_PAIRS // 8) * 8


def _route_kernel(q_ref, keys_ref, idx_ref, gate_ref, cand_ref, cid_ref):
    tm = q_ref.shape[0]
    scores = [_dot_nt(keys_ref[0, p], q_ref[:, p * P_HALF:(p + 1) * P_HALF]) for p in range(2)]
    vals, ids = _top_rows(scores, P_TOPK)
    cand_ref[...] = jnp.full(cand_ref.shape, NEG_INF, F32)
    cid_ref[...] = jnp.zeros(cid_ref.shape, F32)
    r0 = 0
    for a, nb in enumerate(_PAIR_COUNTS):
        cand_ref[r0:r0 + nb, :] = vals[0][a:a + 1] + vals[1][0:nb]
        cid_ref[r0:r0 + nb, :] = ids[0][a:a + 1] * float(N_KEYS) + ids[1][0:nb]
        r0 += nb
    cand = cand_ref[...]
    cid = cid_ref[...]
    n = cand.shape[0]
    rid = lax.broadcasted_iota(jnp.int32, (n, tm), 0).astype(F32)
    kid = lax.broadcasted_iota(jnp.int32, (P_TOPK, tm), 0)

    def body(j, carry):
        cand, top, idx = carry
        m = jnp.max(cand, axis=0, keepdims=True)
        pos = jnp.min(jnp.where(cand == m, rid, float(n)), axis=0, keepdims=True)
        sel = rid == pos
        e = jnp.sum(jnp.where(sel, cid, 0.0), axis=0, keepdims=True)
        top = jnp.where(kid == j, m, top)
        idx = jnp.where(kid == j, e, idx)
        return jnp.where(sel, NEG_INF, cand), top, idx

    _, top, idx = lax.fori_loop(
        0, P_TOPK, body, (cand, jnp.zeros((P_TOPK, tm), F32), jnp.zeros((P_TOPK, tm), F32)))
    ex = jnp.exp(top - top[0:1])
    gate_ref[...] = ex / jnp.sum(ex, axis=0, keepdims=True)
    idx_ref[...] = idx.astype(jnp.int32)


def _route(q, keys):
    t = q.shape[0]
    tm = 256
    return pl.pallas_call(
        _route_kernel,
        grid=(P_HEADS, t // tm),
        in_specs=[
            pl.BlockSpec((tm, 2 * P_HALF), lambda h, i: (i, h)),
            pl.BlockSpec((1, 2, N_KEYS, P_HALF), lambda h, i: (h, 0, 0, 0)),
        ],
        out_specs=[
            pl.BlockSpec((P_TOPK, tm), lambda h, i: (h, i)),
            pl.BlockSpec((P_TOPK, tm), lambda h, i: (h, i)),
        ],
        out_shape=[
            jax.ShapeDtypeStruct((N_SEL, t), jnp.int32),
            jax.ShapeDtypeStruct((N_SEL, t), F32),
        ],
        scratch_shapes=[pltpu.VMEM((_N_PAIR_ROWS, tm), F32), pltpu.VMEM((_N_PAIR_ROWS, tm), F32)],
        compiler_params=_cparams(("parallel", "parallel")),
        name="peer_route",
    )(q, keys)


PEER_TB = 128
PEER_NBUF = 8
SUBLANES = 8
D_TILES = D_MODEL // LANE


def _sublane_sums(p):
    s = lax.broadcasted_iota(jnp.int32, p.shape[2:], 0)
    sel = s < 4
    a, b = p[:, 0:4], p[:, 4:8]
    f = jnp.where(sel, a, b) + pltpu.roll(jnp.where(sel, b, a), 4, 2)
    sel = (s & 2) == 0
    a, b = f[:, 0:2], f[:, 2:4]
    g = jnp.where(sel, a + pltpu.roll(a, 6, 2), b + pltpu.roll(b, 2, 2))
    sel = (s & 1) == 0
    a, b = g[:, 0], g[:, 1]
    return jnp.where(sel, a + pltpu.roll(a, 7, 1), b + pltpu.roll(b, 1, 1))


def _peer_kernel(idx_ref, idx_next_ref, h_ref, gate_ref, x_ref, g2_ref, uv_hbm, o_ref, *scratch):
    bufs, cb_ref, sem = scratch[:PEER_NBUF], scratch[PEER_NBUF], scratch[PEER_NBUF + 1]
    step, nsteps = pl.program_id(0), pl.num_programs(0)

    def start_token(ids_ref, t, slot):
        for e in range(N_SEL):
            pltpu.make_async_copy(uv_hbm.at[ids_ref[t, e]], bufs[slot].at[e], sem.at[slot]).start(priority=e % 2)

    def wait_token(slot):
        pltpu.make_async_copy(uv_hbm.at[pl.ds(0, N_SEL)], bufs[slot], sem.at[slot]).wait()

    eye = lax.broadcasted_iota(jnp.int32, (N_SEL, N_SEL), 0) == lax.broadcasted_iota(jnp.int32, (N_SEL, N_SEL), 1)
    g2 = g2_ref[0]

    def compute(t, slot):
        buf = bufs[slot]
        prod = buf[:, 0:SUBLANES, :] * h_ref[t]
        rows = _sublane_sums(prod.reshape(N_SEL // 8, 8, SUBLANES, LANE))
        s = jnp.sum(rows.reshape(N_SEL, LANE), axis=1, keepdims=True)
        gcol = jnp.sum(jnp.where(eye, gate_ref[pl.ds(t, 1), :], 0.0), axis=1, keepdims=True)
        cb_ref[...] = jnp.broadcast_to(gcol * _gelu(s), (N_SEL, LANE))
        accs = [jnp.zeros((SUBLANES, LANE), F32) for _ in range(4)]
        for e in range(N_SEL):
            accs[e % 4] = accs[e % 4] + cb_ref[pl.ds(e, 1), :] * buf[e, SUBLANES:2 * SUBLANES, :]
        o_ref[t] = x_ref[t] + g2 * ((accs[0] + accs[1]) + (accs[2] + accs[3]))

    ahead = PEER_NBUF - 1

    @pl.when(step == 0)
    def _():
        for t in range(ahead):
            start_token(idx_ref, t, t)

    def group(t0, last):
        for k in range(PEER_NBUF):
            wait_token(k)
            nxt, slot = t0 + k + ahead, (k + ahead) % PEER_NBUF
            if not last:
                start_token(idx_ref, nxt, slot)
            elif k + ahead < PEER_NBUF:
                start_token(idx_ref, PEER_TB - PEER_NBUF + k + ahead, slot)
            else:
                @pl.when(step + 1 < nsteps)
                def _():
                    start_token(idx_next_ref, k + ahead - PEER_NBUF, slot)
            compute(t0 + k, k)

    def body(i, _):
        group(PEER_NBUF * i, False)
        return 0

    lax.fori_loop(0, PEER_TB // PEER_NBUF - 1, body, 0)
    group(PEER_TB - PEER_NBUF, True)


def _peer(idx_t, gate_t, h, x, mod3, stream, layer, uv3):
    t, d = x.shape
    tb = PEER_TB
    st = _Stream(stream.n_prompt_blocks * stream.tm, stream.blocks_per_sample * stream.tm, tb)
    tok_spec = pl.BlockSpec((tb, D_TILES, LANE), lambda i: (i, 0, 0))
    last_block = t // tb - 1
    out = pl.pallas_call(
        _peer_kernel,
        grid=(t // tb,),
        in_specs=[
            pl.BlockSpec((tb, N_SEL), lambda i: (i, 0), memory_space=pltpu.SMEM),
            pl.BlockSpec((tb, N_SEL), lambda i: (jnp.minimum(i + 1, last_block), 0), memory_space=pltpu.SMEM),
            tok_spec,
            pl.BlockSpec((tb, N_SEL), lambda i: (i, 0)),
            tok_spec,
            pl.BlockSpec((1, D_TILES, LANE), st.mod_index(layer, 5)),
            pl.BlockSpec(memory_space=pl.ANY),
        ],
        out_specs=tok_spec,
        out_shape=jax.ShapeDtypeStruct((t, D_TILES, LANE), F32),
        scratch_shapes=[pltpu.VMEM((N_SEL, 2 * SUBLANES, LANE), F32) for _ in range(PEER_NBUF)] + [
            pltpu.VMEM((N_SEL, LANE), F32),
            pltpu.SemaphoreType.DMA((PEER_NBUF,)),
        ],
        compiler_params=_cparams(("arbitrary",)),
        name="peer_experts",
    )(idx_t, idx_t, h.reshape(t, D_TILES, LANE), gate_t, x.reshape(t, D_TILES, LANE),
      mod3.reshape(-1, D_TILES, LANE), uv3)
    return out.reshape(t, d)


def _rms_kernel(x_ref, w_ref, o_ref):
    x = x_ref[...]
    ms = jnp.mean(x * x, axis=-1, keepdims=True)
    o_ref[...] = x * lax.rsqrt(ms + EPS) * w_ref[...]


def _rms_norm(x, w, row0, rows):
    d = x.shape[1]
    tm = 512
    assert row0 % tm == 0 and rows % tm == 0
    return pl.pallas_call(
        _rms_kernel,
        grid=(rows // tm,),
        in_specs=[pl.BlockSpec((tm, d), lambda i: (row0 // tm + i, 0)), pl.BlockSpec((1, d), lambda i: (0, 0))],
        out_specs=pl.BlockSpec((tm, d), lambda i: (i, 0)),
        out_shape=jax.ShapeDtypeStruct((rows, d), F32),
        compiler_params=_cparams(("parallel",)),
        name="final_norm",
    )(x, w.reshape(1, d))


def kernel(x_prompt, x_sample, state_hgrn, state_gla, c, c_ctx, ada_w, ada_b, norm1_w, norm2_w, final_norm_w, hgrn_w_in, hgrn_lb, hgrn_gnorm_w, hgrn_w_out, gla_w_in, gla_conv_w, gla_w_gate, gla_b_gate, gla_gnorm_w, gla_w_out, peer_w_q, peer_keys, peer_u, peer_v):
    bp, lp, d = x_prompt.shape
    bs, ls, _ = x_sample.shape
    t_p, t_s = bp * lp, bs * ls
    rows_s = ls // GRID_W
    stream = _Stream(t_p, ls, ROW_TILE)

    x = jnp.concatenate([x_prompt.reshape(t_p, d), x_sample.reshape(t_s, d)], axis=0)

    cvec = jnp.concatenate([c_ctx[None, :], c, jnp.zeros((MOD_ROWS - 1 - bs, d), F32)], axis=0)
    mod3 = _adaln(cvec, ada_w, ada_b).reshape(DEPTH * MOD_ROWS * 6, 1, d)

    p = jax.nn.softmax(hgrn_lb.astype(F32), axis=0)
    lower_bounds = jnp.cumsum(p, axis=0) - p[0:1]

    gla_n = gla_w_in.shape[2]
    gla_np = -(-gla_n // LANE) * LANE
    low_col_block = (QKV_B + D_MODEL) // LANE
    n_b = gla_w_in.shape[0]
    wg_pad = jnp.zeros((n_b, 2, LANE, H_B * DK_B), F32)
    wg_pad = wg_pad.at[:, 0, 0:GLA_RANK].set(gla_w_gate[:, 0]).at[:, 1, GLA_RANK:2 * GLA_RANK].set(gla_w_gate[:, 1])

    o_zero = jnp.zeros((t_p + t_s, d), F32)

    new_hgrn, new_gla = [], []
    for l in range(DEPTH):
        j = l // 2
        if l % 2 == 0:
            z = _norm_mod_matmul(x, norm1_w[l], mod3, stream, l, 0, 1, hgrn_w_in[j].astype(BF16), 1024)
            lb = lower_bounds[j]
            of_p, ob_p, sf, sb = _hgrn_scan(z, lb, None, 0, bp, lp, prev_out=(o_zero, o_zero))
            init = (jnp.swapaxes(state_hgrn[:, j, 0], -1, -2), jnp.swapaxes(state_hgrn[:, j, 1], -1, -2))
            o_f, o_b, _, _ = _hgrn_scan(z, lb, init, t_p, bs, ls, prev_out=(of_p, ob_p))
            new_hgrn.append(jnp.stack([jnp.swapaxes(sf, -1, -2), jnp.swapaxes(sb, -1, -2)], axis=1))
            heads, dvh, og_blk = H_A, DV_A, 4
            gnw, w_out = hgrn_gnorm_w[j], hgrn_w_out[j]
        else:
            w_in = jnp.pad(gla_w_in[j], ((0, 0), (0, gla_np - gla_n))).astype(BF16)
            z = _norm_mod_matmul(x, norm1_w[l], mod3, stream, l, 0, 1, w_in, gla_np // 5 if gla_np % 5 == 0 else LANE)
            cw = gla_conv_w[j].reshape(9, QKV_B)
            y_p = _conv_silu(z, cw, 0, bp, 1, lp)
            y_s = _conv_silu(z, cw, t_p, bs, rows_s, GRID_W)
            wg, bg = wg_pad[j], gla_b_gate[j]
            of_p, ob_p, sf, sb = _gla_scan(y_p, z, wg, bg, None, 0, bp, lp, low_col_block, prev_out=(o_zero, o_zero))
            init = (jnp.swapaxes(state_gla[:, j, 0], -1, -2), jnp.swapaxes(state_gla[:, j, 1], -1, -2))
            o_f, o_b, _, _ = _gla_scan(y_s, z, wg, bg, init, t_p, bs, ls, low_col_block, prev_out=(of_p, ob_p))
            new_gla.append(jnp.stack([jnp.swapaxes(sf, -1, -2), jnp.swapaxes(sb, -1, -2)], axis=1))
            heads, dvh, og_blk = H_B, DV_B, 2
            gnw, w_out = gla_gnorm_w[j], gla_w_out[j]
        x = _outproj(o_f, o_b, z, og_blk, gnw, w_out.astype(BF16), x, mod3, stream, l, heads, dvh)

        q, h = _norm_mod_matmul(x, norm2_w[l], mod3, stream, l, 3, 4, peer_w_q[l].astype(BF16), 1024, emit_h=True)
        idx, gates = _route(q, peer_keys[l])
        n_exp = peer_u.shape[1]
        uv3 = jnp.concatenate([peer_u[l].reshape(n_exp, D_TILES, LANE), peer_v[l].reshape(n_exp, D_TILES, LANE)], axis=1)
        x = _peer(idx.T, gates.T, h, x, mod3, stream, l, uv3)

    y_prompt = _rms_norm(x, final_norm_w, 0, t_p).reshape(bp, lp, d)
    y_sample = _rms_norm(x, final_norm_w, t_p, t_s).reshape(bs, ls, d)
    return (y_prompt, y_sample, jnp.stack(new_hgrn, axis=1), jnp.stack(new_gla, axis=1))
```

```python
import functools

import jax
import jax.numpy as jnp
from jax import lax
from jax.experimental import pallas as pl
from jax.experimental.pallas import tpu as pltpu

F32 = jnp.float32
BF16 = jnp.bfloat16

D_MODEL = 1024
DEPTH = 4
GRID_W = 64
EPS = 1e-6
F_MIN = 1e-20
H_A, DK_A, DV_A = 8, 128, 128
H_B, DK_B, DV_B = 4, 128, 256
QKV_B = 2048
GLA_RANK = 16
GATE_NORM = 16.0
P_HEADS, N_KEYS, P_TOPK, P_HALF = 8, 128, 16, 128
N_SEL = P_HEADS * P_TOPK

V7X_VMEM_LIMIT = 48 * 1024 * 1024
LANE = 128
ROW_TILE = 1024
CHUNK = 64
SUB = 16
MOD_ROWS = 16
NEG_INF = float("-inf")


def _cparams(sem):
    return pltpu.CompilerParams(dimension_semantics=sem, vmem_limit_bytes=V7X_VMEM_LIMIT)


def _silu(x):
    return x * jax.nn.sigmoid(x)


def _gelu(x):
    return 0.5 * x * (1.0 + lax.erf(x * (2.0 ** -0.5)))


def _dot(a, b):
    return jnp.dot(a, b, preferred_element_type=F32)


def _dot_nt(a, b):
    return lax.dot_general(a, b, (((1,), (1,)), ((), ())), preferred_element_type=F32)


def _dot_tn(a, b):
    return lax.dot_general(a, b, (((0,), (0,)), ((), ())), preferred_element_type=F32)


def _dot_01(m01, x):
    x1 = x.astype(BF16)
    r1 = x - x1.astype(F32)
    x2 = r1.astype(BF16)
    x3 = (r1 - x2.astype(F32)).astype(BF16)
    return _dot(m01, x1) + _dot(m01, x2) + _dot(m01, x3)


def _adaln_kernel(c_ref, w_ref, b_ref, o_ref):
    o_ref[0] = _dot(_silu(c_ref[...]), w_ref[0]) + b_ref[0]


def _adaln(cvec, ada_w, ada_b):
    tn = 1536
    n = 6 * D_MODEL
    return pl.pallas_call(
        _adaln_kernel,
        grid=(DEPTH, n // tn),
        in_specs=[
            pl.BlockSpec((MOD_ROWS, D_MODEL), lambda l, j: (0, 0)),
            pl.BlockSpec((1, D_MODEL, tn), lambda l, j: (l, 0, j)),
            pl.BlockSpec((1, 1, tn), lambda l, j: (l, 0, j)),
        ],
        out_specs=pl.BlockSpec((1, MOD_ROWS, tn), lambda l, j: (l, 0, j)),
        out_shape=jax.ShapeDtypeStruct((DEPTH, MOD_ROWS, n), F32),
        compiler_params=_cparams(("parallel", "parallel")),
        name="adaln",
    )(cvec, ada_w, ada_b.reshape(DEPTH, 1, n))


class _Stream:
    def __init__(self, t_prompt, seq_sample, tm):
        assert t_prompt % tm == 0 and seq_sample % tm == 0
        self.tm = tm
        self.n_prompt_blocks = t_prompt // tm
        self.blocks_per_sample = seq_sample // tm

    def mod_index(self, layer, k):
        npb, bps = self.n_prompt_blocks, self.blocks_per_sample

        def f(i, *_):
            row = jnp.where(i < npb, 0, 1 + (i - npb) // bps)
            return ((layer * MOD_ROWS + row) * 6 + k, 0, 0)

        return f


def _norm_mod(x, nw, shift, scale):
    ms = jnp.mean(x * x, axis=-1, keepdims=True)
    return (x * lax.rsqrt(ms + EPS) * nw) * (1.0 + scale) + shift


def _nmm_kernel(emit_h, x_ref, nw_ref, sh_ref, sc_ref, w_ref, *rest):
    if emit_h:
        o_ref, h_ref, hs_ref = rest
    else:
        o_ref, hs_ref = rest

    @pl.when(pl.program_id(1) == 0)
    def _():
        h = _norm_mod(x_ref[...], nw_ref[...], sh_ref[0], sc_ref[0])
        hs_ref[...] = h.astype(BF16)
        if emit_h:
            h_ref[...] = h

    o_ref[...] = _dot(hs_ref[...], w_ref[...])


def _norm_mod_matmul(x, nw, mod3, stream, layer, k_shift, k_scale, w_bf16, tn, emit_h=False):
    t, d = x.shape
    n = w_bf16.shape[1]
    tm = stream.tm
    assert n % tn == 0
    out_shape = [jax.ShapeDtypeStruct((t, n), F32)]
    out_specs = [pl.BlockSpec((tm, tn), lambda i, j: (i, j))]
    if emit_h:
        out_shape.append(jax.ShapeDtypeStruct((t, d), F32))
        out_specs.append(pl.BlockSpec((tm, d), lambda i, j: (i, 0)))
    res = pl.pallas_call(
        functools.partial(_nmm_kernel, emit_h),
        grid=(t // tm, n // tn),
        in_specs=[
            pl.BlockSpec((tm, d), lambda i, j: (i, 0)),
            pl.BlockSpec((1, d), lambda i, j: (0, 0)),
            pl.BlockSpec((1, 1, d), stream.mod_index(layer, k_shift)),
            pl.BlockSpec((1, 1, d), stream.mod_index(layer, k_scale)),
            pl.BlockSpec((d, tn), lambda i, j: (0, j)),
        ],
        out_specs=out_specs,
        out_shape=out_shape,
        scratch_shapes=[pltpu.VMEM((tm, d), BF16)],
        compiler_params=_cparams(("parallel", "arbitrary")),
        name="norm_mod_matmul",
    )(x, nw.reshape(1, d), mod3, mod3, w_bf16)
    return res if emit_h else res[0]


def _scan_direction(q, k, v, g, st_ref, sidx, rev):
    c = CHUNK
    row = lax.broadcasted_iota(jnp.int32, (c, c), 0)
    col = lax.broadcasted_iota(jnp.int32, (c, c), 1)
    cmat = jnp.where((col >= row) if rev else (col <= row), 1.0, 0.0).astype(BF16)
    cs = _dot_01(cmat, g)
    st = st_ref[sidx]
    o = _dot_nt(q * jnp.exp(cs), st)
    edge = cs[0:1] if rev else cs[c - 1:c]
    st_ref[sidx] = st * jnp.exp(edge) + _dot_tn(v, k * jnp.exp(edge - cs))

    srow = lax.broadcasted_iota(jnp.int32, (SUB, c), 0)
    scol = lax.broadcasted_iota(jnp.int32, (SUB, c), 1)
    att_rows = []
    for blk in range(c // SUB):
        r0 = blk * SUB
        q_i, k_i, cs_i = q[r0:r0 + SUB], k[r0:r0 + SUB], cs[r0:r0 + SUB]
        has_off = (blk < c // SUB - 1) if rev else (blk > 0)
        if has_off:
            ref = cs[r0 + SUB:r0 + SUB + 1] if rev else cs[r0 - 1:r0]
            w_i = cs_i - ref
            kt = k * jnp.exp(jnp.minimum(ref - cs, 0.0))
            off = _dot_nt(q_i * jnp.exp(w_i), kt)
            off_mask = (scol >= r0 + SUB) if rev else (scol < r0)
            a_i = jnp.where(off_mask, off, 0.0)
        else:
            w_i = cs_i
            a_i = jnp.zeros((SUB, c), F32)
        for s in range(SUB):
            e = jnp.exp(jnp.minimum(w_i - w_i[s:s + 1], 0.0))
            colv = jnp.sum(q_i * e * k_i[s:s + 1], axis=1, keepdims=True)
            keep = (srow <= s) if rev else (srow >= s)
            a_i = jnp.where((scol == r0 + s) & keep, colv, a_i)
        att_rows.append(a_i)
    att = jnp.concatenate(att_rows, axis=0)
    return o + _dot(att, v)


def _log_sigmoid(x):
    return jnp.minimum(x, 0.0) - jnp.log(1.0 + jnp.exp(-jnp.abs(x)))


SCAN_HPB = 8


def _scan_kernel(mode, zero_init, n_aliased, hpb, dk, dv, *refs):
    if mode == "hgrn":
        (qf, vf, ff, qb, vb, fb, lbf, lbb), rest = refs[:8], refs[8:]
    else:
        (qf, kf, vf, lowf, qb, kb, vb, lowb, wgf, wgb, bgf, bgb), rest = refs[:12], refs[12:]
    if not zero_init:
        (s0f, s0b), rest = rest[:2], rest[2:]
    rest = rest[n_aliased:]
    of_ref, ob_ref, sf_ref, sb_ref, st_ref = rest
    ci = pl.program_id(2)

    @pl.when(ci == 0)
    def _():
        if zero_init:
            st_ref[...] = jnp.zeros_like(st_ref)
        else:
            for j in range(hpb):
                st_ref[j] = s0f[0, j]
                st_ref[hpb + j] = s0b[0, j]

    def prep(ks, q_ref, k_ref, f_ref, lb_ref, low_ref, wg_ref, bg_ref):
        if mode == "hgrn":
            lb = lb_ref[:, ks]
            fp = f_ref[:, ks]
            f = lb + (1.0 - lb) * jax.nn.sigmoid(fp)
            g = jnp.log(jnp.maximum(f, F_MIN))
            k = (1.0 - lb) * jax.nn.sigmoid(-fp)
            q = _silu(q_ref[:, ks])
        else:
            gk = _dot(low_ref[...], wg_ref[:, ks]) + bg_ref[:, ks]
            g = _log_sigmoid(gk) * (1.0 / GATE_NORM)
            k = k_ref[:, ks]
            q = q_ref[:, ks] * (DK_B ** -0.5)
        return q, k, g

    for j in range(hpb):
        ks, vs = slice(j * dk, (j + 1) * dk), slice(j * dv, (j + 1) * dv)
        if mode == "hgrn":
            q, k, g = prep(ks, qf, None, ff, lbf, None, None, None)
            of_ref[:, vs] = _scan_direction(q, k, vf[:, vs], g, st_ref, j, False)
            q, k, g = prep(ks, qb, None, fb, lbb, None, None, None)
            ob_ref[:, vs] = _scan_direction(q, k, vb[:, vs], g, st_ref, hpb + j, True)
        else:
            q, k, g = prep(ks, qf, kf, None, None, lowf, wgf, bgf)
            of_ref[:, vs] = _scan_direction(q, k, vf[:, vs], g, st_ref, j, False)
            q, k, g = prep(ks, qb, kb, None, None, lowb, wgb, bgb)
            ob_ref[:, vs] = _scan_direction(q, k, vb[:, vs], g, st_ref, hpb + j, True)

    @pl.when(ci == pl.num_programs(2) - 1)
    def _():
        for j in range(hpb):
            sf_ref[0, j] = st_ref[j]
            sb_ref[0, j] = st_ref[hpb + j]


def _scan(mode, arrays, consts, init_states, batch, seq, heads, dk, dv, out_rows, out_row0, prev_out):
    n = seq // CHUNK
    hpb = min(SCAN_HPB, heads)
    assert heads % hpb == 0 and out_row0 % CHUNK == 0
    zero_init = init_states is None
    orb0 = out_row0 // CHUNK

    def chunk_of(c, dirn):
        return c if dirn == 0 else n - 1 - c

    in_specs, operands = [], []
    for dirn in (0, 1):
        for arr, width, row0, cols, per_head in arrays:
            assert row0 % CHUNK == 0 and (not per_head or cols[dirn] % hpb == 0)

            def imap(b, hg, c, dirn=dirn, rb0=row0 // CHUNK, col0=cols[dirn], per_head=per_head):
                return (rb0 + b * n + chunk_of(c, dirn), (col0 // hpb + hg) if per_head else col0)

            in_specs.append(pl.BlockSpec((CHUNK, width * hpb if per_head else width), imap))
            operands.append(arr)
    for arr, shape in consts:
        in_specs.append(pl.BlockSpec((shape[0], shape[1] * hpb), lambda b, hg, c: (0, hg)))
        operands.append(arr)
    s_spec = pl.BlockSpec((1, hpb, dv, dk), lambda b, hg, c: (b, hg, 0, 0))
    if not zero_init:
        for s0 in init_states:
            in_specs.append(s_spec)
            operands.append(s0)
    aliases = {}
    if prev_out is not None:
        for k, arr in enumerate(prev_out):
            aliases[len(operands)] = k
            in_specs.append(pl.BlockSpec(memory_space=pl.ANY))
            operands.append(arr)
    o_specs = [pl.BlockSpec((CHUNK, dv * hpb), lambda b, hg, c, dirn=dirn: (orb0 + b * n + chunk_of(c, dirn), hg))
               for dirn in (0, 1)]
    return pl.pallas_call(
        functools.partial(_scan_kernel, mode, zero_init, len(aliases), hpb, dk, dv),
        grid=(batch, heads // hpb, n),
        in_specs=in_specs,
        out_specs=o_specs + [s_spec, s_spec],
        out_shape=[
            jax.ShapeDtypeStruct((out_rows, heads * dv), F32),
            jax.ShapeDtypeStruct((out_rows, heads * dv), F32),
            jax.ShapeDtypeStruct((batch, heads, dv, dk), F32),
            jax.ShapeDtypeStruct((batch, heads, dv, dk), F32),
        ],
        scratch_shapes=[pltpu.VMEM((2 * hpb, dv, dk), F32)],
        input_output_aliases=aliases,
        compiler_params=_cparams(("parallel", "parallel", "arbitrary")),
        name="scan_" + mode,
    )(*operands)


def _hgrn_scan(z, lb, init_states, row0, batch, seq, prev_out=None):
    nh = H_A
    arrays = [
        (z, DK_A, row0, (0, 0), True),
        (z, DV_A, row0, (nh, nh), True),
        (z, DK_A, row0, (2 * nh, 3 * nh), True),
    ]
    consts = [(lb[0:1], (1, DK_A)), (lb[1:2], (1, DK_A))]
    return _scan("hgrn", arrays, consts, init_states, batch, seq, H_A, DK_A, DV_A, z.shape[0], row0, prev_out)


def _gla_scan(y, z, wg, bg, init_states, row0, batch, seq, low_col_block, prev_out=None):
    v0 = 2 * H_B * DK_B // DV_B
    arrays = [
        (y, DK_B, 0, (0, 0), True),
        (y, DK_B, 0, (H_B, H_B), True),
        (y, DV_B, 0, (v0, v0), True),
        (z, LANE, row0, (low_col_block, low_col_block), False),
    ]
    consts = [(wg[0], (LANE, DK_B)), (wg[1], (LANE, DK_B)), (bg[0:1], (1, DK_B)), (bg[1:2], (1, DK_B))]
    return _scan("gla", arrays, consts, init_states, batch, seq, H_B, DK_B, DV_B, z.shape[0], row0, prev_out)


def _conv_kernel(rows, width, x_ref, w_ref, o_ref):
    seq = rows * width
    x = x_ref[...]
    assert width & (width - 1) == 0
    l = lax.broadcasted_iota(jnp.int32, (seq, 1), 0)
    r, c = l >> (width.bit_length() - 1), l & (width - 1)
    acc = jnp.zeros_like(x)
    for di in range(3):
        if rows == 1 and di != 1:
            continue
        for dj in range(3):
            k = (di - 1) * width + (dj - 1)
            xs = x if k == 0 else pltpu.roll(x, (-k) % seq, 0)
            ok = (r + (di - 1) >= 0) & (r + (di - 1) < rows) & (c + (dj - 1) >= 0) & (c + (dj - 1) < width)
            acc = acc + jnp.where(ok, xs, 0.0) * w_ref[3 * di + dj:3 * di + dj + 1, :]
    o_ref[...] = _silu(acc)


def _conv_silu(z, conv_w9, row0, batch, rows, width, tc=256):
    seq = rows * width
    sb0 = row0 // seq
    assert row0 % seq == 0
    return pl.pallas_call(
        functools.partial(_conv_kernel, rows, width),
        grid=(batch, QKV_B // tc),
        in_specs=[
            pl.BlockSpec((seq, tc), lambda b, j: (sb0 + b, j)),
            pl.BlockSpec((9, tc), lambda b, j: (0, j)),
        ],
        out_specs=pl.BlockSpec((seq, tc), lambda b, j: (b, j)),
        out_shape=jax.ShapeDtypeStruct((batch * seq, QKV_B), F32),
        compiler_params=_cparams(("parallel", "parallel")),
        name="conv_silu",
    )(z, conv_w9)


def _outproj_kernel(heads, dvh, of_ref, ob_ref, og_ref, gw_ref, w_ref, x_ref, g_ref, o_ref):
    o = of_ref[...] + ob_ref[...]
    og = og_ref[...]
    gw = gw_ref[...]
    parts = []
    for h in range(heads):
        oh = o[:, h * dvh:(h + 1) * dvh]
        ms = jnp.mean(oh * oh, axis=-1, keepdims=True)
        parts.append(oh * lax.rsqrt(ms + EPS) * gw)
    y = jnp.concatenate(parts, axis=1) * _silu(og)
    o_ref[...] = x_ref[...] + g_ref[0] * _dot(y.astype(BF16), w_ref[...])


def _outproj(o_f, o_b, z, og_col_block, gnorm_w, w_out_bf16, x, mod3, stream, layer, heads, dvh):
    t, d = x.shape
    tm = 256
    st = _Stream(stream.n_prompt_blocks * stream.tm, stream.blocks_per_sample * stream.tm, tm)
    return pl.pallas_call(
        functools.partial(_outproj_kernel, heads, dvh),
        grid=(t // tm,),
        in_specs=[
            pl.BlockSpec((tm, d), lambda i: (i, 0)),
            pl.BlockSpec((tm, d), lambda i: (i, 0)),
            pl.BlockSpec((tm, d), lambda i: (i, og_col_block)),
            pl.BlockSpec((1, dvh), lambda i: (0, 0)),
            pl.BlockSpec((d, d), lambda i: (0, 0)),
            pl.BlockSpec((tm, d), lambda i: (i, 0)),
            pl.BlockSpec((1, 1, d), st.mod_index(layer, 2)),
        ],
        out_specs=pl.BlockSpec((tm, d), lambda i: (i, 0)),
        out_shape=jax.ShapeDtypeStruct((t, d), F32),
        compiler_params=_cparams(("parallel",)),
        name="outproj",
    )(o_f, o_b, z, gnorm_w.reshape(1, dvh), w_out_bf16, x, mod3)


def _top_rows(scores, k):
    n, tm = scores[0].shape
    rid = lax.broadcasted_iota(jnp.int32, (n, tm), 0).astype(F32)
    kid = lax.broadcasted_iota(jnp.int32, (k, tm), 0)

    def body(j, carry):
        out = []
        for s, vals, ids in carry:
            m = jnp.max(s, axis=0, keepdims=True)
            pos = jnp.min(jnp.where(s == m, rid, float(n)), axis=0, keepdims=True)
            vals = jnp.where(kid == j, m, vals)
            ids = jnp.where(kid == j, pos, ids)
            out.append((jnp.where(rid == pos, NEG_INF, s), vals, ids))
        return tuple(out)

    init = tuple((s, jnp.zeros((k, tm), F32), jnp.zeros((k, tm), F32)) for s in scores)
    res = lax.fori_loop(0, k, body, init)
    return [r[1] for r in res], [r[2] for r in res]


_PAIR_COUNTS = [P_TOPK // (a + 1) for a in range(P_TOPK)]
_N_PAIRS = sum(_PAIR_COUNTS)
_N_PAIR_ROWS = -(-_N_PAIRS // 8) * 8


ROUTE_TOKENS = 512
ROUTE_SUB = 256


def _route_kernel(q_ref, keys_ref, idx_ref, gate_ref, cand_ref, cid_ref):
    for c0 in range(0, q_ref.shape[0], ROUTE_SUB):
        _route_tile(q_ref.at[c0:c0 + ROUTE_SUB], keys_ref, idx_ref.at[:, c0:c0 + ROUTE_SUB],
                    gate_ref.at[:, c0:c0 + ROUTE_SUB], cand_ref, cid_ref)


def _route_tile(q_ref, keys_ref, idx_ref, gate_ref, cand_ref, cid_ref):
    tm = q_ref.shape[0]
    scores = [_dot_nt(keys_ref[0, p], q_ref[:, p * P_HALF:(p + 1) * P_HALF]) for p in range(2)]
    vals, ids = _top_rows(scores, P_TOPK)
    cand_ref[...] = jnp.full(cand_ref.shape, NEG_INF, F32)
    cid_ref[...] = jnp.zeros(cid_ref.shape, F32)
    r0 = 0
    for a, nb in enumerate(_PAIR_COUNTS):
        cand_ref[r0:r0 + nb, :] = vals[0][a:a + 1] + vals[1][0:nb]
        cid_ref[r0:r0 + nb, :] = ids[0][a:a + 1] * float(N_KEYS) + ids[1][0:nb]
        r0 += nb
    cand = cand_ref[...]
    cid = cid_ref[...]
    n = cand.shape[0]
    rid = lax.broadcasted_iota(jnp.int32, (n, tm), 0).astype(F32)
    kid = lax.broadcasted_iota(jnp.int32, (P_TOPK, tm), 0)

    def body(j, carry):
        cand, top, idx = carry
        m = jnp.max(cand, axis=0, keepdims=True)
        pos = jnp.min(jnp.where(cand == m, rid, float(n)), axis=0, keepdims=True)
        sel = rid == pos
        e = jnp.sum(jnp.where(sel, cid, 0.0), axis=0, keepdims=True)
        top = jnp.where(kid == j, m, top)
        idx = jnp.where(kid == j, e, idx)
        return jnp.where(sel, NEG_INF, cand), top, idx

    _, top, idx = lax.fori_loop(
        0, P_TOPK, body, (cand, jnp.zeros((P_TOPK, tm), F32), jnp.zeros((P_TOPK, tm), F32)))
    ex = jnp.exp(top - top[0:1])
    gate_ref[...] = ex / jnp.sum(ex, axis=0, keepdims=True)
    idx_ref[...] = idx.astype(jnp.int32)


def _route(q, keys):
    t = q.shape[0]
    tm = ROUTE_TOKENS
    return pl.pallas_call(
        _route_kernel,
        grid=(P_HEADS, t // tm),
        in_specs=[
            pl.BlockSpec((tm, 2 * P_HALF), lambda h, i: (i, h)),
            pl.BlockSpec((1, 2, N_KEYS, P_HALF), lambda h, i: (h, 0, 0, 0)),
        ],
        out_specs=[
            pl.BlockSpec((P_TOPK, tm), lambda h, i: (h, i)),
            pl.BlockSpec((P_TOPK, tm), lambda h, i: (h, i)),
        ],
        out_shape=[
            jax.ShapeDtypeStruct((N_SEL, t), jnp.int32),
            jax.ShapeDtypeStruct((N_SEL, t), F32),
        ],
        scratch_shapes=[pltpu.VMEM((_N_PAIR_ROWS, ROUTE_SUB), F32), pltpu.VMEM((_N_PAIR_ROWS, ROUTE_SUB), F32)],
        compiler_params=_cparams(("parallel", "parallel")),
        name="peer_route",
    )(q, keys)


PEER_TB = 128
PEER_NBUF = 8
SUBLANES = 8
D_TILES = D_MODEL // LANE


def _sublane_sums(p):
    s = lax.broadcasted_iota(jnp.int32, p.shape[2:], 0)
    sel = s < 4
    a, b = p[:, 0:4], p[:, 4:8]
    f = jnp.where(sel, a, b) + pltpu.roll(jnp.where(sel, b, a), 4, 2)
    sel = (s & 2) == 0
    a, b = f[:, 0:2], f[:, 2:4]
    g = jnp.where(sel, a + pltpu.roll(a, 6, 2), b + pltpu.roll(b, 2, 2))
    sel = (s & 1) == 0
    a, b = g[:, 0], g[:, 1]
    return jnp.where(sel, a + pltpu.roll(a, 7, 1), b + pltpu.roll(b, 1, 1))


def _peer_kernel(idx_ref, idx_next_ref, h_ref, gate_ref, x_ref, g2_ref, uv_hbm, o_ref, *scratch):
    bufs, cb_ref, sem = scratch[:PEER_NBUF], scratch[PEER_NBUF], scratch[PEER_NBUF + 1]
    step, nsteps = pl.program_id(0), pl.num_programs(0)

    def start_token(ids_ref, t, slot):
        for e in range(N_SEL):
            pltpu.make_async_copy(uv_hbm.at[ids_ref[t, e]], bufs[slot].at[e], sem.at[slot]).start(priority=e % 2)

    def wait_token(slot):
        pltpu.make_async_copy(uv_hbm.at[pl.ds(0, N_SEL)], bufs[slot], sem.at[slot]).wait()

    eye = lax.broadcasted_iota(jnp.int32, (N_SEL, N_SEL), 0) == lax.broadcasted_iota(jnp.int32, (N_SEL, N_SEL), 1)
    g2 = g2_ref[0]

    def compute(t, slot):
        buf = bufs[slot]
        prod = buf[:, 0:SUBLANES, :] * h_ref[t]
        rows = _sublane_sums(prod.reshape(N_SEL // 8, 8, SUBLANES, LANE))
        s = jnp.sum(rows.reshape(N_SEL, LANE), axis=1, keepdims=True)
        gcol = jnp.sum(jnp.where(eye, gate_ref[pl.ds(t, 1), :], 0.0), axis=1, keepdims=True)
        cb_ref[...] = jnp.broadcast_to(gcol * _gelu(s), (N_SEL, LANE))
        accs = [jnp.zeros((SUBLANES, LANE), F32) for _ in range(4)]
        for e in range(N_SEL):
            accs[e % 4] = accs[e % 4] + cb_ref[pl.ds(e, 1), :] * buf[e, SUBLANES:2 * SUBLANES, :]
        o_ref[t] = x_ref[t] + g2 * ((accs[0] + accs[1]) + (accs[2] + accs[3]))

    ahead = PEER_NBUF - 1

    @pl.when(step == 0)
    def _():
        for t in range(ahead):
            start_token(idx_ref, t, t)

    def group(t0, last):
        for k in range(PEER_NBUF):
            wait_token(k)
            nxt, slot = t0 + k + ahead, (k + ahead) % PEER_NBUF
            if not last:
                start_token(idx_ref, nxt, slot)
            elif k + ahead < PEER_NBUF:
                start_token(idx_ref, PEER_TB - PEER_NBUF + k + ahead, slot)
            else:
                @pl.when(step + 1 < nsteps)
                def _():
                    start_token(idx_next_ref, k + ahead - PEER_NBUF, slot)
            compute(t0 + k, k)

    def body(i, _):
        group(PEER_NBUF * i, False)
        return 0

    lax.fori_loop(0, PEER_TB // PEER_NBUF - 1, body, 0)
    group(PEER_TB - PEER_NBUF, True)


def _peer(idx_t, gate_t, h, x, mod3, stream, layer, uv3):
    t, d = x.shape
    tb = PEER_TB
    st = _Stream(stream.n_prompt_blocks * stream.tm, stream.blocks_per_sample * stream.tm, tb)
    tok_spec = pl.BlockSpec((tb, D_TILES, LANE), lambda i: (i, 0, 0))
    last_block = t // tb - 1
    out = pl.pallas_call(
        _peer_kernel,
        grid=(t // tb,),
        in_specs=[
            pl.BlockSpec((tb, N_SEL), lambda i: (i, 0), memory_space=pltpu.SMEM),
            pl.BlockSpec((tb, N_SEL), lambda i: (jnp.minimum(i + 1, last_block), 0), memory_space=pltpu.SMEM),
            tok_spec,
            pl.BlockSpec((tb, N_SEL), lambda i: (i, 0)),
            tok_spec,
            pl.BlockSpec((1, D_TILES, LANE), st.mod_index(layer, 5)),
            pl.BlockSpec(memory_space=pl.ANY),
        ],
        out_specs=tok_spec,
        out_shape=jax.ShapeDtypeStruct((t, D_TILES, LANE), F32),
        scratch_shapes=[pltpu.VMEM((N_SEL, 2 * SUBLANES, LANE), F32) for _ in range(PEER_NBUF)] + [
            pltpu.VMEM((N_SEL, LANE), F32),
            pltpu.SemaphoreType.DMA((PEER_NBUF,)),
        ],
        compiler_params=_cparams(("arbitrary",)),
        name="peer_experts",
    )(idx_t, idx_t, h.reshape(t, D_TILES, LANE), gate_t, x.reshape(t, D_TILES, LANE),
      mod3.reshape(-1, D_TILES, LANE), uv3)
    return out.reshape(t, d)


def _rms_kernel(x_ref, w_ref, o_ref):
    x = x_ref[...]
    ms = jnp.mean(x * x, axis=-1, keepdims=True)
    o_ref[...] = x * lax.rsqrt(ms + EPS) * w_ref[...]


def _rms_norm(x, w, row0, rows):
    d = x.shape[1]
    tm = 512
    assert row0 % tm == 0 and rows % tm == 0
    return pl.pallas_call(
        _rms_kernel,
        grid=(rows // tm,),
        in_specs=[pl.BlockSpec((tm, d), lambda i: (row0 // tm + i, 0)), pl.BlockSpec((1, d), lambda i: (0, 0))],
        out_specs=pl.BlockSpec((tm, d), lambda i: (i, 0)),
        out_shape=jax.ShapeDtypeStruct((rows, d), F32),
        compiler_params=_cparams(("parallel",)),
        name="final_norm",
    )(x, w.reshape(1, d))


def kernel(x_prompt, x_sample, state_hgrn, state_gla, c, c_ctx, ada_w, ada_b, norm1_w, norm2_w, final_norm_w, hgrn_w_in, hgrn_lb, hgrn_gnorm_w, hgrn_w_out, gla_w_in, gla_conv_w, gla_w_gate, gla_b_gate, gla_gnorm_w, gla_w_out, peer_w_q, peer_keys, peer_u, peer_v):
    bp, lp, d = x_prompt.shape
    bs, ls, _ = x_sample.shape
    t_p, t_s = bp * lp, bs * ls
    rows_s = ls // GRID_W
    stream = _Stream(t_p, ls, ROW_TILE)

    x = jnp.concatenate([x_prompt.reshape(t_p, d), x_sample.reshape(t_s, d)], axis=0)

    cvec = jnp.concatenate([c_ctx[None, :], c, jnp.zeros((MOD_ROWS - 1 - bs, d), F32)], axis=0)
    mod3 = _adaln(cvec, ada_w, ada_b).reshape(DEPTH * MOD_ROWS * 6, 1, d)

    p = jax.nn.softmax(hgrn_lb.astype(F32), axis=0)
    lower_bounds = jnp.cumsum(p, axis=0) - p[0:1]

    gla_n = gla_w_in.shape[2]
    gla_np = -(-gla_n // LANE) * LANE
    low_col_block = (QKV_B + D_MODEL) // LANE
    n_b = gla_w_in.shape[0]
    wg_pad = jnp.zeros((n_b, 2, LANE, H_B * DK_B), F32)
    wg_pad = wg_pad.at[:, 0, 0:GLA_RANK].set(gla_w_gate[:, 0]).at[:, 1, GLA_RANK:2 * GLA_RANK].set(gla_w_gate[:, 1])

    o_zero = jnp.zeros((t_p + t_s, d), F32)

    new_hgrn, new_gla = [], []
    for l in range(DEPTH):
        j = l // 2
        if l % 2 == 0:
            z = _norm_mod_matmul(x, norm1_w[l], mod3, stream, l, 0, 1, hgrn_w_in[j].astype(BF16), 1024)
            lb = lower_bounds[j]
            of_p, ob_p, sf, sb = _hgrn_scan(z, lb, None, 0, bp, lp, prev_out=(o_zero, o_zero))
            init = (jnp.swapaxes(state_hgrn[:, j, 0], -1, -2), jnp.swapaxes(state_hgrn[:, j, 1], -1, -2))
            o_f, o_b, _, _ = _hgrn_scan(z, lb, init, t_p, bs, ls, prev_out=(of_p, ob_p))
            new_hgrn.append(jnp.stack([jnp.swapaxes(sf, -1, -2), jnp.swapaxes(sb, -1, -2)], axis=1))
            heads, dvh, og_blk = H_A, DV_A, 4
            gnw, w_out = hgrn_gnorm_w[j], hgrn_w_out[j]
        else:
            w_in = jnp.pad(gla_w_in[j], ((0, 0), (0, gla_np - gla_n))).astype(BF16)
            z = _norm_mod_matmul(x, norm1_w[l], mod3, stream, l, 0, 1, w_in, gla_np // 5 if gla_np % 5 == 0 else LANE)
            cw = gla_conv_w[j].reshape(9, QKV_B)
            y_p = _conv_silu(z, cw, 0, bp, 1, lp)
            y_s = _conv_silu(z, cw, t_p, bs, rows_s, GRID_W)
            wg, bg = wg_pad[j], gla_b_gate[j]
            of_p, ob_p, sf, sb = _gla_scan(y_p, z, wg, bg, None, 0, bp, lp, low_col_block, prev_out=(o_zero, o_zero))
            init = (jnp.swapaxes(state_gla[:, j, 0], -1, -2), jnp.swapaxes(state_gla[:, j, 1], -1, -2))
            o_f, o_b, _, _ = _gla_scan(y_s, z, wg, bg, init, t_p, bs, ls, low_col_block, prev_out=(of_p, ob_p))
            new_gla.append(jnp.stack([jnp.swapaxes(sf, -1, -2), jnp.swapaxes(sb, -1, -2)], axis=1))
            heads, dvh, og_blk = H_B, DV_B, 2
            gnw, w_out = gla_gnorm_w[j], gla_w_out[j]
        x = _outproj(o_f, o_b, z, og_blk, gnw, w_out.astype(BF16), x, mod3, stream, l, heads, dvh)

        q, h = _norm_mod_matmul(x, norm2_w[l], mod3, stream, l, 3, 4, peer_w_q[l].astype(BF16), 1024, emit_h=True)
        idx, gates = _route(q, peer_keys[l])
        n_exp = peer_u.shape[1]
        uv3 = jnp.concatenate([peer_u[l].reshape(n_exp, D_TILES, LANE), peer_v[l].reshape(n_exp, D_TILES, LANE)], axis=1)
        x = _peer(idx.T, gates.T, h, x, mod3, stream, l, uv3)

    y_prompt = _rms_norm(x, final_norm_w, 0, t_p).reshape(bp, lp, d)
    y_sample = _rms_norm(x, final_norm_w, t_p, t_s).reshape(bs, ls, d)
    return (y_prompt, y_sample, jnp.stack(new_hgrn, axis=1), jnp.stack(new_gla, axis=1))
```

```python
import functools

import jax
import jax.numpy as jnp
from jax import lax
from jax.experimental import pallas as pl
from jax.experimental.pallas import tpu as pltpu

F32 = jnp.float32
BF16 = jnp.bfloat16

D_MODEL = 1024
DEPTH = 4
GRID_W = 64
EPS = 1e-6
F_MIN = 1e-20
H_A, DK_A, DV_A = 8, 128, 128
H_B, DK_B, DV_B = 4, 128, 256
QKV_B = 2048
GLA_RANK = 16
GATE_NORM = 16.0
P_HEADS, N_KEYS, P_TOPK, P_HALF = 8, 128, 16, 128
N_SEL = P_HEADS * P_TOPK

V7X_VMEM_LIMIT = 48 * 1024 * 1024
LANE = 128
ROW_TILE = 1024
CHUNK = 64
SUB = 16
MOD_ROWS = 16
NEG_INF = float("-inf")


def _cparams(sem):
    return pltpu.CompilerParams(dimension_semantics=sem, vmem_limit_bytes=V7X_VMEM_LIMIT)


def _silu(x):
    return x * jax.nn.sigmoid(x)


def _gelu(x):
    return 0.5 * x * (1.0 + lax.erf(x * (2.0 ** -0.5)))


def _dot(a, b):
    return jnp.dot(a, b, preferred_element_type=F32)


def _dot_nt(a, b):
    return lax.dot_general(a, b, (((1,), (1,)), ((), ())), preferred_element_type=F32)


def _dot_tn(a, b):
    return lax.dot_general(a, b, (((0,), (0,)), ((), ())), preferred_element_type=F32)


def _dot_01(m01, x):
    x1 = x.astype(BF16)
    r1 = x - x1.astype(F32)
    x2 = r1.astype(BF16)
    x3 = (r1 - x2.astype(F32)).astype(BF16)
    return _dot(m01, x1) + _dot(m01, x2) + _dot(m01, x3)


def _adaln_kernel(c_ref, w_ref, b_ref, o_ref):
    o_ref[0] = _dot(_silu(c_ref[...]), w_ref[0]) + b_ref[0]


def _adaln(cvec, ada_w, ada_b):
    tn = 1536
    n = 6 * D_MODEL
    return pl.pallas_call(
        _adaln_kernel,
        grid=(DEPTH, n // tn),
        in_specs=[
            pl.BlockSpec((MOD_ROWS, D_MODEL), lambda l, j: (0, 0)),
            pl.BlockSpec((1, D_MODEL, tn), lambda l, j: (l, 0, j)),
            pl.BlockSpec((1, 1, tn), lambda l, j: (l, 0, j)),
        ],
        out_specs=pl.BlockSpec((1, MOD_ROWS, tn), lambda l, j: (l, 0, j)),
        out_shape=jax.ShapeDtypeStruct((DEPTH, MOD_ROWS, n), F32),
        compiler_params=_cparams(("parallel", "parallel")),
        name="adaln",
    )(cvec, ada_w, ada_b.reshape(DEPTH, 1, n))


class _Stream:
    def __init__(self, t_prompt, seq_sample, tm):
        assert t_prompt % tm == 0 and seq_sample % tm == 0
        self.tm = tm
        self.n_prompt_blocks = t_prompt // tm
        self.blocks_per_sample = seq_sample // tm

    def mod_index(self, layer, k):
        npb, bps = self.n_prompt_blocks, self.blocks_per_sample

        def f(i, *_):
            row = jnp.where(i < npb, 0, 1 + (i - npb) // bps)
            return ((layer * MOD_ROWS + row) * 6 + k, 0, 0)

        return f


def _norm_mod(x, nw, shift, scale):
    ms = jnp.mean(x * x, axis=-1, keepdims=True)
    return (x * lax.rsqrt(ms + EPS) * nw) * (1.0 + scale) + shift


def _nmm_kernel(emit_h, x_ref, nw_ref, sh_ref, sc_ref, w_ref, *rest):
    if emit_h:
        o_ref, h_ref, hs_ref = rest
    else:
        o_ref, hs_ref = rest

    @pl.when(pl.program_id(1) == 0)
    def _():
        h = _norm_mod(x_ref[...], nw_ref[...], sh_ref[0], sc_ref[0])
        hs_ref[...] = h.astype(BF16)
        if emit_h:
            h_ref[...] = h

    o_ref[...] = _dot(hs_ref[...], w_ref[...])


def _norm_mod_matmul(x, nw, mod3, stream, layer, k_shift, k_scale, w_bf16, tn, emit_h=False):
    t, d = x.shape
    n = w_bf16.shape[1]
    tm = stream.tm
    assert n % tn == 0
    out_shape = [jax.ShapeDtypeStruct((t, n), F32)]
    out_specs = [pl.BlockSpec((tm, tn), lambda i, j: (i, j))]
    if emit_h:
        out_shape.append(jax.ShapeDtypeStruct((t, d), F32))
        out_specs.append(pl.BlockSpec((tm, d), lambda i, j: (i, 0)))
    res = pl.pallas_call(
        functools.partial(_nmm_kernel, emit_h),
        grid=(t // tm, n // tn),
        in_specs=[
            pl.BlockSpec((tm, d), lambda i, j: (i, 0)),
            pl.BlockSpec((1, d), lambda i, j: (0, 0)),
            pl.BlockSpec((1, 1, d), stream.mod_index(layer, k_shift)),
            pl.BlockSpec((1, 1, d), stream.mod_index(layer, k_scale)),
            pl.BlockSpec((d, tn), lambda i, j: (0, j)),
        ],
        out_specs=out_specs,
        out_shape=out_shape,
        scratch_shapes=[pltpu.VMEM((tm, d), BF16)],
        compiler_params=_cparams(("parallel", "arbitrary")),
        name="norm_mod_matmul",
    )(x, nw.reshape(1, d), mod3, mod3, w_bf16)
    return res if emit_h else res[0]


def _scan_direction(q, k, v, g, st_ref, sidx, rev):
    c = CHUNK
    row = lax.broadcasted_iota(jnp.int32, (c, c), 0)
    col = lax.broadcasted_iota(jnp.int32, (c, c), 1)
    cmat = jnp.where((col >= row) if rev else (col <= row), 1.0, 0.0).astype(BF16)
    cs = _dot_01(cmat, g)
    st = st_ref[sidx]
    o = _dot_nt(q * jnp.exp(cs), st)
    edge = cs[0:1] if rev else cs[c - 1:c]
    st_ref[sidx] = st * jnp.exp(edge) + _dot_tn(v, k * jnp.exp(edge - cs))

    srow = lax.broadcasted_iota(jnp.int32, (SUB, c), 0)
    scol = lax.broadcasted_iota(jnp.int32, (SUB, c), 1)
    att_rows = []
    for blk in range(c // SUB):
        r0 = blk * SUB
        q_i, k_i, cs_i = q[r0:r0 + SUB], k[r0:r0 + SUB], cs[r0:r0 + SUB]
        has_off = (blk < c // SUB - 1) if rev else (blk > 0)
        if has_off:
            ref = cs[r0 + SUB:r0 + SUB + 1] if rev else cs[r0 - 1:r0]
            w_i = cs_i - ref
            kt = k * jnp.exp(jnp.minimum(ref - cs, 0.0))
            off = _dot_nt(q_i * jnp.exp(w_i), kt)
            off_mask = (scol >= r0 + SUB) if rev else (scol < r0)
            a_i = jnp.where(off_mask, off, 0.0)
        else:
            w_i = cs_i
            a_i = jnp.zeros((SUB, c), F32)
        for s in range(SUB):
            e = jnp.exp(jnp.minimum(w_i - w_i[s:s + 1], 0.0))
            colv = jnp.sum(q_i * e * k_i[s:s + 1], axis=1, keepdims=True)
            keep = (srow <= s) if rev else (srow >= s)
            a_i = jnp.where((scol == r0 + s) & keep, colv, a_i)
        att_rows.append(a_i)
    att = jnp.concatenate(att_rows, axis=0)
    return o + _dot(att, v)


def _log_sigmoid(x):
    return jnp.minimum(x, 0.0) - jnp.log(1.0 + jnp.exp(-jnp.abs(x)))


SCAN_HPB = 8


def _scan_kernel(mode, zero_init, n_aliased, hpb, dk, dv, *refs):
    if mode == "hgrn":
        (qf, vf, ff, qb, vb, fb, lbf, lbb), rest = refs[:8], refs[8:]
    else:
        (qf, kf, vf, lowf, qb, kb, vb, lowb, wgf, wgb, bgf, bgb), rest = refs[:12], refs[12:]
    if not zero_init:
        (s0f, s0b), rest = rest[:2], rest[2:]
    rest = rest[n_aliased:]
    of_ref, ob_ref, sf_ref, sb_ref, st_ref = rest
    ci = pl.program_id(2)

    @pl.when(ci == 0)
    def _():
        if zero_init:
            st_ref[...] = jnp.zeros_like(st_ref)
        else:
            for j in range(hpb):
                st_ref[j] = s0f[0, j]
                st_ref[hpb + j] = s0b[0, j]

    def prep(ks, q_ref, k_ref, f_ref, lb_ref, low_ref, wg_ref, bg_ref):
        if mode == "hgrn":
            lb = lb_ref[:, ks]
            fp = f_ref[:, ks]
            f = lb + (1.0 - lb) * jax.nn.sigmoid(fp)
            g = jnp.log(jnp.maximum(f, F_MIN))
            k = (1.0 - lb) * jax.nn.sigmoid(-fp)
            q = _silu(q_ref[:, ks])
        else:
            gk = _dot(low_ref[...], wg_ref[:, ks]) + bg_ref[:, ks]
            g = _log_sigmoid(gk) * (1.0 / GATE_NORM)
            k = k_ref[:, ks]
            q = q_ref[:, ks] * (DK_B ** -0.5)
        return q, k, g

    for j in range(hpb):
        ks, vs = slice(j * dk, (j + 1) * dk), slice(j * dv, (j + 1) * dv)
        if mode == "hgrn":
            q, k, g = prep(ks, qf, None, ff, lbf, None, None, None)
            of_ref[:, vs] = _scan_direction(q, k, vf[:, vs], g, st_ref, j, False)
            q, k, g = prep(ks, qb, None, fb, lbb, None, None, None)
            ob_ref[:, vs] = _scan_direction(q, k, vb[:, vs], g, st_ref, hpb + j, True)
        else:
            q, k, g = prep(ks, qf, kf, None, None, lowf, wgf, bgf)
            of_ref[:, vs] = _scan_direction(q, k, vf[:, vs], g, st_ref, j, False)
            q, k, g = prep(ks, qb, kb, None, None, lowb, wgb, bgb)
            ob_ref[:, vs] = _scan_direction(q, k, vb[:, vs], g, st_ref, hpb + j, True)

    @pl.when(ci == pl.num_programs(2) - 1)
    def _():
        for j in range(hpb):
            sf_ref[0, j] = st_ref[j]
            sb_ref[0, j] = st_ref[hpb + j]


def _scan(mode, arrays, consts, init_states, batch, seq, heads, dk, dv, out_rows, out_row0, prev_out):
    n = seq // CHUNK
    hpb = min(SCAN_HPB, heads)
    assert heads % hpb == 0 and out_row0 % CHUNK == 0
    zero_init = init_states is None
    orb0 = out_row0 // CHUNK

    def chunk_of(c, dirn):
        return c if dirn == 0 else n - 1 - c

    in_specs, operands = [], []
    for dirn in (0, 1):
        for arr, width, row0, cols, per_head in arrays:
            assert row0 % CHUNK == 0 and (not per_head or cols[dirn] % hpb == 0)

            def imap(b, hg, c, dirn=dirn, rb0=row0 // CHUNK, col0=cols[dirn], per_head=per_head):
                return (rb0 + b * n + chunk_of(c, dirn), (col0 // hpb + hg) if per_head else col0)

            in_specs.append(pl.BlockSpec((CHUNK, width * hpb if per_head else width), imap))
            operands.append(arr)
    for arr, shape in consts:
        in_specs.append(pl.BlockSpec((shape[0], shape[1] * hpb), lambda b, hg, c: (0, hg)))
        operands.append(arr)
    s_spec = pl.BlockSpec((1, hpb, dv, dk), lambda b, hg, c: (b, hg, 0, 0))
    if not zero_init:
        for s0 in init_states:
            in_specs.append(s_spec)
            operands.append(s0)
    aliases = {}
    if prev_out is not None:
        for k, arr in enumerate(prev_out):
            aliases[len(operands)] = k
            in_specs.append(pl.BlockSpec(memory_space=pl.ANY))
            operands.append(arr)
    o_specs = [pl.BlockSpec((CHUNK, dv * hpb), lambda b, hg, c, dirn=dirn: (orb0 + b * n + chunk_of(c, dirn), hg))
               for dirn in (0, 1)]
    return pl.pallas_call(
        functools.partial(_scan_kernel, mode, zero_init, len(aliases), hpb, dk, dv),
        grid=(batch, heads // hpb, n),
        in_specs=in_specs,
        out_specs=o_specs + [s_spec, s_spec],
        out_shape=[
            jax.ShapeDtypeStruct((out_rows, heads * dv), F32),
            jax.ShapeDtypeStruct((out_rows, heads * dv), F32),
            jax.ShapeDtypeStruct((batch, heads, dv, dk), F32),
            jax.ShapeDtypeStruct((batch, heads, dv, dk), F32),
        ],
        scratch_shapes=[pltpu.VMEM((2 * hpb, dv, dk), F32)],
        input_output_aliases=aliases,
        compiler_params=_cparams(("parallel", "parallel", "arbitrary")),
        name="scan_" + mode,
    )(*operands)


def _hgrn_scan(z, lb, init_states, row0, batch, seq, prev_out=None):
    nh = H_A
    arrays = [
        (z, DK_A, row0, (0, 0), True),
        (z, DV_A, row0, (nh, nh), True),
        (z, DK_A, row0, (2 * nh, 3 * nh), True),
    ]
    consts = [(lb[0:1], (1, DK_A)), (lb[1:2], (1, DK_A))]
    return _scan("hgrn", arrays, consts, init_states, batch, seq, H_A, DK_A, DV_A, z.shape[0], row0, prev_out)


def _gla_scan(y, z, wg, bg, init_states, row0, batch, seq, low_col_block, prev_out=None):
    v0 = 2 * H_B * DK_B // DV_B
    arrays = [
        (y, DK_B, 0, (0, 0), True),
        (y, DK_B, 0, (H_B, H_B), True),
        (y, DV_B, 0, (v0, v0), True),
        (z, LANE, row0, (low_col_block, low_col_block), False),
    ]
    consts = [(wg[0], (LANE, DK_B)), (wg[1], (LANE, DK_B)), (bg[0:1], (1, DK_B)), (bg[1:2], (1, DK_B))]
    return _scan("gla", arrays, consts, init_states, batch, seq, H_B, DK_B, DV_B, z.shape[0], row0, prev_out)


def _conv_kernel(rows, width, x_ref, w_ref, o_ref):
    seq = rows * width
    x = x_ref[...]
    assert width & (width - 1) == 0
    l = lax.broadcasted_iota(jnp.int32, (seq, 1), 0)
    r, c = l >> (width.bit_length() - 1), l & (width - 1)
    acc = jnp.zeros_like(x)
    for di in range(3):
        if rows == 1 and di != 1:
            continue
        for dj in range(3):
            k = (di - 1) * width + (dj - 1)
            xs = x if k == 0 else pltpu.roll(x, (-k) % seq, 0)
            ok = (r + (di - 1) >= 0) & (r + (di - 1) < rows) & (c + (dj - 1) >= 0) & (c + (dj - 1) < width)
            acc = acc + jnp.where(ok, xs, 0.0) * w_ref[3 * di + dj:3 * di + dj + 1, :]
    o_ref[...] = _silu(acc)


def _conv_silu(z, conv_w9, row0, batch, rows, width, tc=256):
    seq = rows * width
    sb0 = row0 // seq
    assert row0 % seq == 0
    return pl.pallas_call(
        functools.partial(_conv_kernel, rows, width),
        grid=(batch, QKV_B // tc),
        in_specs=[
            pl.BlockSpec((seq, tc), lambda b, j: (sb0 + b, j)),
            pl.BlockSpec((9, tc), lambda b, j: (0, j)),
        ],
        out_specs=pl.BlockSpec((seq, tc), lambda b, j: (b, j)),
        out_shape=jax.ShapeDtypeStruct((batch * seq, QKV_B), F32),
        compiler_params=_cparams(("parallel", "parallel")),
        name="conv_silu",
    )(z, conv_w9)


def _outproj_kernel(heads, dvh, of_ref, ob_ref, og_ref, gw_ref, w_ref, x_ref, g_ref, o_ref):
    o = of_ref[...] + ob_ref[...]
    og = og_ref[...]
    gw = gw_ref[...]
    parts = []
    for h in range(heads):
        oh = o[:, h * dvh:(h + 1) * dvh]
        ms = jnp.mean(oh * oh, axis=-1, keepdims=True)
        parts.append(oh * lax.rsqrt(ms + EPS) * gw)
    y = jnp.concatenate(parts, axis=1) * _silu(og)
    o_ref[...] = x_ref[...] + g_ref[0] * _dot(y.astype(BF16), w_ref[...])


def _outproj(o_f, o_b, z, og_col_block, gnorm_w, w_out_bf16, x, mod3, stream, layer, heads, dvh):
    t, d = x.shape
    tm = 256
    st = _Stream(stream.n_prompt_blocks * stream.tm, stream.blocks_per_sample * stream.tm, tm)
    return pl.pallas_call(
        functools.partial(_outproj_kernel, heads, dvh),
        grid=(t // tm,),
        in_specs=[
            pl.BlockSpec((tm, d), lambda i: (i, 0)),
            pl.BlockSpec((tm, d), lambda i: (i, 0)),
            pl.BlockSpec((tm, d), lambda i: (i, og_col_block)),
            pl.BlockSpec((1, dvh), lambda i: (0, 0)),
            pl.BlockSpec((d, d), lambda i: (0, 0)),
            pl.BlockSpec((tm, d), lambda i: (i, 0)),
            pl.BlockSpec((1, 1, d), st.mod_index(layer, 2)),
        ],
        out_specs=pl.BlockSpec((tm, d), lambda i: (i, 0)),
        out_shape=jax.ShapeDtypeStruct((t, d), F32),
        compiler_params=_cparams(("parallel",)),
        name="outproj",
    )(o_f, o_b, z, gnorm_w.reshape(1, dvh), w_out_bf16, x, mod3)


def _top_rows(scores, k):
    n, tm = scores[0].shape
    rid = lax.broadcasted_iota(jnp.int32, (n, tm), 0).astype(F32)
    kid = lax.broadcasted_iota(jnp.int32, (k, tm), 0)

    def body(j, carry):
        out = []
        for s, vals, ids in carry:
            m = jnp.max(s, axis=0, keepdims=True)
            pos = jnp.min(jnp.where(s == m, rid, float(n)), axis=0, keepdims=True)
            vals = jnp.where(kid == j, m, vals)
            ids = jnp.where(kid == j, pos, ids)
            out.append((jnp.where(rid == pos, NEG_INF, s), vals, ids))
        return tuple(out)

    init = tuple((s, jnp.zeros((k, tm), F32), jnp.zeros((k, tm), F32)) for s in scores)
    res = lax.fori_loop(0, k, body, init)
    return [r[1] for r in res], [r[2] for r in res]


_PAIR_COUNTS = [P_TOPK // (a + 1) for a in range(P_TOPK)]
_N_PAIRS = sum(_PAIR_COUNTS)
_N_PAIR_ROWS = -(-_N_PAIRS // 8) * 8


ROUTE_TOKENS = 512
ROUTE_SUB = 256


def _route_kernel(q_ref, keys_ref, idx_ref, gate_ref, cand_ref, cid_ref):
    for c0 in range(0, q_ref.shape[0], ROUTE_SUB):
        _route_tile(q_ref.at[c0:c0 + ROUTE_SUB], keys_ref, idx_ref.at[:, c0:c0 + ROUTE_SUB],
                    gate_ref.at[:, c0:c0 + ROUTE_SUB], cand_ref, cid_ref)


def _route_tile(q_ref, keys_ref, idx_ref, gate_ref, cand_ref, cid_ref):
    tm = q_ref.shape[0]
    scores = [_dot_nt(keys_ref[0, p], q_ref[:, p * P_HALF:(p + 1) * P_HALF]) for p in range(2)]
    vals, ids = _top_rows(scores, P_TOPK)
    cand_ref[...] = jnp.full(cand_ref.shape, NEG_INF, F32)
    cid_ref[...] = jnp.zeros(cid_ref.shape, F32)
    r0 = 0
    for a, nb in enumerate(_PAIR_COUNTS):
        cand_ref[r0:r0 + nb, :] = vals[0][a:a + 1] + vals[1][0:nb]
        cid_ref[r0:r0 + nb, :] = ids[0][a:a + 1] * float(N_KEYS) + ids[1][0:nb]
        r0 += nb
    cand = cand_ref[...]
    cid = cid_ref[...]
    n = cand.shape[0]
    rid = lax.broadcasted_iota(jnp.int32, (n, tm), 0).astype(F32)
    kid = lax.broadcasted_iota(jnp.int32, (P_TOPK, tm), 0)

    def body(j, carry):
        cand, top, idx = carry
        m = jnp.max(cand, axis=0, keepdims=True)
        pos = jnp.min(jnp.where(cand == m, rid, float(n)), axis=0, keepdims=True)
        sel = rid == pos
        e = jnp.sum(jnp.where(sel, cid, 0.0), axis=0, keepdims=True)
        top = jnp.where(kid == j, m, top)
        idx = jnp.where(kid == j, e, idx)
        return jnp.where(sel, NEG_INF, cand), top, idx

    _, top, idx = lax.fori_loop(
        0, P_TOPK, body, (cand, jnp.zeros((P_TOPK, tm), F32), jnp.zeros((P_TOPK, tm), F32)))
    ex = jnp.exp(top - top[0:1])
    gate_ref[...] = ex / jnp.sum(ex, axis=0, keepdims=True)
    idx_ref[...] = idx.astype(jnp.int32)


def _route(q, keys):
    t = q.shape[0]
    tm = ROUTE_TOKENS
    return pl.pallas_call(
        _route_kernel,
        grid=(P_HEADS, t // tm),
        in_specs=[
            pl.BlockSpec((tm, 2 * P_HALF), lambda h, i: (i, h)),
            pl.BlockSpec((1, 2, N_KEYS, P_HALF), lambda h, i: (h, 0, 0, 0)),
        ],
        out_specs=[
            pl.BlockSpec((P_TOPK, tm), lambda h, i: (h, i)),
            pl.BlockSpec((P_TOPK, tm), lambda h, i: (h, i)),
        ],
        out_shape=[
            jax.ShapeDtypeStruct((N_SEL, t), jnp.int32),
            jax.ShapeDtypeStruct((N_SEL, t), F32),
        ],
        scratch_shapes=[pltpu.VMEM((_N_PAIR_ROWS, ROUTE_SUB), F32), pltpu.VMEM((_N_PAIR_ROWS, ROUTE_SUB), F32)],
        compiler_params=_cparams(("parallel", "parallel")),
        name="peer_route",
    )(q, keys)


PEER_TB = 128
PEER_NBUF = 16
SUBLANES = 8
D_TILES = D_MODEL // LANE


def _sublane_sums(p):
    s = lax.broadcasted_iota(jnp.int32, p.shape[2:], 0)
    sel = s < 4
    a, b = p[:, 0:4], p[:, 4:8]
    f = jnp.where(sel, a, b) + pltpu.roll(jnp.where(sel, b, a), 4, 2)
    sel = (s & 2) == 0
    a, b = f[:, 0:2], f[:, 2:4]
    g = jnp.where(sel, a + pltpu.roll(a, 6, 2), b + pltpu.roll(b, 2, 2))
    sel = (s & 1) == 0
    a, b = g[:, 0], g[:, 1]
    return jnp.where(sel, a + pltpu.roll(a, 7, 1), b + pltpu.roll(b, 1, 1))


def _peer_kernel(idx_ref, idx_next_ref, h_ref, gate_ref, x_ref, g2_ref, uv_hbm, o_ref, *scratch):
    bufs, cb_ref, sem = scratch[:PEER_NBUF], scratch[PEER_NBUF], scratch[PEER_NBUF + 1]
    step, nsteps = pl.program_id(0), pl.num_programs(0)

    def start_token(ids_ref, t, slot):
        for e in range(N_SEL):
            pltpu.make_async_copy(uv_hbm.at[ids_ref[t, e]], bufs[slot].at[e], sem.at[slot]).start(priority=e % 2)

    def wait_token(slot):
        pltpu.make_async_copy(uv_hbm.at[pl.ds(0, N_SEL)], bufs[slot], sem.at[slot]).wait()

    eye = lax.broadcasted_iota(jnp.int32, (N_SEL, N_SEL), 0) == lax.broadcasted_iota(jnp.int32, (N_SEL, N_SEL), 1)
    g2 = g2_ref[0]

    def compute(t, slot):
        buf = bufs[slot]
        prod = buf[:, 0:SUBLANES, :] * h_ref[t]
        rows = _sublane_sums(prod.reshape(N_SEL // 8, 8, SUBLANES, LANE))
        s = jnp.sum(rows.reshape(N_SEL, LANE), axis=1, keepdims=True)
        gcol = jnp.sum(jnp.where(eye, gate_ref[pl.ds(t, 1), :], 0.0), axis=1, keepdims=True)
        cb_ref[...] = jnp.broadcast_to(gcol * _gelu(s), (N_SEL, LANE))
        accs = [jnp.zeros((SUBLANES, LANE), F32) for _ in range(4)]
        for e in range(N_SEL):
            accs[e % 4] = accs[e % 4] + cb_ref[pl.ds(e, 1), :] * buf[e, SUBLANES:2 * SUBLANES, :]
        o_ref[t] = x_ref[t] + g2 * ((accs[0] + accs[1]) + (accs[2] + accs[3]))

    ahead = PEER_NBUF - 1

    @pl.when(step == 0)
    def _():
        for t in range(ahead):
            start_token(idx_ref, t, t)

    def group(t0, last):
        for k in range(PEER_NBUF):
            wait_token(k)
            nxt, slot = t0 + k + ahead, (k + ahead) % PEER_NBUF
            if not last:
                start_token(idx_ref, nxt, slot)
            elif k + ahead < PEER_NBUF:
                start_token(idx_ref, PEER_TB - PEER_NBUF + k + ahead, slot)
            else:
                @pl.when(step + 1 < nsteps)
                def _():
                    start_token(idx_next_ref, k + ahead - PEER_NBUF, slot)
            compute(t0 + k, k)

    def body(i, _):
        group(PEER_NBUF * i, False)
        return 0

    lax.fori_loop(0, PEER_TB // PEER_NBUF - 1, body, 0)
    group(PEER_TB - PEER_NBUF, True)


def _peer(idx_t, gate_t, h, x, mod3, stream, layer, uv3):
    t, d = x.shape
    tb = PEER_TB
    st = _Stream(stream.n_prompt_blocks * stream.tm, stream.blocks_per_sample * stream.tm, tb)
    tok_spec = pl.BlockSpec((tb, D_TILES, LANE), lambda i: (i, 0, 0))
    last_block = t // tb - 1
    out = pl.pallas_call(
        _peer_kernel,
        grid=(t // tb,),
        in_specs=[
            pl.BlockSpec((tb, N_SEL), lambda i: (i, 0), memory_space=pltpu.SMEM),
            pl.BlockSpec((tb, N_SEL), lambda i: (jnp.minimum(i + 1, last_block), 0), memory_space=pltpu.SMEM),
            tok_spec,
            pl.BlockSpec((tb, N_SEL), lambda i: (i, 0)),
            tok_spec,
            pl.BlockSpec((1, D_TILES, LANE), st.mod_index(layer, 5)),
            pl.BlockSpec(memory_space=pl.ANY),
        ],
        out_specs=tok_spec,
        out_shape=jax.ShapeDtypeStruct((t, D_TILES, LANE), F32),
        scratch_shapes=[pltpu.VMEM((N_SEL, 2 * SUBLANES, LANE), F32) for _ in range(PEER_NBUF)] + [
            pltpu.VMEM((N_SEL, LANE), F32),
            pltpu.SemaphoreType.DMA((PEER_NBUF,)),
        ],
        compiler_params=_cparams(("arbitrary",)),
        name="peer_experts",
    )(idx_t, idx_t, h.reshape(t, D_TILES, LANE), gate_t, x.reshape(t, D_TILES, LANE),
      mod3.reshape(-1, D_TILES, LANE), uv3)
    return out.reshape(t, d)


def _rms_kernel(x_ref, w_ref, o_ref):
    x = x_ref[...]
    ms = jnp.mean(x * x, axis=-1, keepdims=True)
    o_ref[...] = x * lax.rsqrt(ms + EPS) * w_ref[...]


def _rms_norm(x, w, row0, rows):
    d = x.shape[1]
    tm = 512
    assert row0 % tm == 0 and rows % tm == 0
    return pl.pallas_call(
        _rms_kernel,
        grid=(rows // tm,),
        in_specs=[pl.BlockSpec((tm, d), lambda i: (row0 // tm + i, 0)), pl.BlockSpec((1, d), lambda i: (0, 0))],
        out_specs=pl.BlockSpec((tm, d), lambda i: (i, 0)),
        out_shape=jax.ShapeDtypeStruct((rows, d), F32),
        compiler_params=_cparams(("parallel",)),
        name="final_norm",
    )(x, w.reshape(1, d))


def kernel(x_prompt, x_sample, state_hgrn, state_gla, c, c_ctx, ada_w, ada_b, norm1_w, norm2_w, final_norm_w, hgrn_w_in, hgrn_lb, hgrn_gnorm_w, hgrn_w_out, gla_w_in, gla_conv_w, gla_w_gate, gla_b_gate, gla_gnorm_w, gla_w_out, peer_w_q, peer_keys, peer_u, peer_v):
    bp, lp, d = x_prompt.shape
    bs, ls, _ = x_sample.shape
    t_p, t_s = bp * lp, bs * ls
    rows_s = ls // GRID_W
    stream = _Stream(t_p, ls, ROW_TILE)

    x = jnp.concatenate([x_prompt.reshape(t_p, d), x_sample.reshape(t_s, d)], axis=0)

    cvec = jnp.concatenate([c_ctx[None, :], c, jnp.zeros((MOD_ROWS - 1 - bs, d), F32)], axis=0)
    mod3 = _adaln(cvec, ada_w, ada_b).reshape(DEPTH * MOD_ROWS * 6, 1, d)

    p = jax.nn.softmax(hgrn_lb.astype(F32), axis=0)
    lower_bounds = jnp.cumsum(p, axis=0) - p[0:1]

    gla_n = gla_w_in.shape[2]
    gla_np = -(-gla_n // LANE) * LANE
    low_col_block = (QKV_B + D_MODEL) // LANE
    n_b = gla_w_in.shape[0]
    wg_pad = jnp.zeros((n_b, 2, LANE, H_B * DK_B), F32)
    wg_pad = wg_pad.at[:, 0, 0:GLA_RANK].set(gla_w_gate[:, 0]).at[:, 1, GLA_RANK:2 * GLA_RANK].set(gla_w_gate[:, 1])

    o_zero = jnp.zeros((t_p + t_s, d), F32)

    new_hgrn, new_gla = [], []
    for l in range(DEPTH):
        j = l // 2
        if l % 2 == 0:
            z = _norm_mod_matmul(x, norm1_w[l], mod3, stream, l, 0, 1, hgrn_w_in[j].astype(BF16), 1024)
            lb = lower_bounds[j]
            of_p, ob_p, sf, sb = _hgrn_scan(z, lb, None, 0, bp, lp, prev_out=(o_zero, o_zero))
            init = (jnp.swapaxes(state_hgrn[:, j, 0], -1, -2), jnp.swapaxes(state_hgrn[:, j, 1], -1, -2))
            o_f, o_b, _, _ = _hgrn_scan(z, lb, init, t_p, bs, ls, prev_out=(of_p, ob_p))
            new_hgrn.append(jnp.stack([jnp.swapaxes(sf, -1, -2), jnp.swapaxes(sb, -1, -2)], axis=1))
            heads, dvh, og_blk = H_A, DV_A, 4
            gnw, w_out = hgrn_gnorm_w[j], hgrn_w_out[j]
        else:
            w_in = jnp.pad(gla_w_in[j], ((0, 0), (0, gla_np - gla_n))).astype(BF16)
            z = _norm_mod_matmul(x, norm1_w[l], mod3, stream, l, 0, 1, w_in, gla_np // 5 if gla_np % 5 == 0 else LANE)
            cw = gla_conv_w[j].reshape(9, QKV_B)
            y_p = _conv_silu(z, cw, 0, bp, 1, lp)
            y_s = _conv_silu(z, cw, t_p, bs, rows_s, GRID_W)
            wg, bg = wg_pad[j], gla_b_gate[j]
            of_p, ob_p, sf, sb = _gla_scan(y_p, z, wg, bg, None, 0, bp, lp, low_col_block, prev_out=(o_zero, o_zero))
            init = (jnp.swapaxes(state_gla[:, j, 0], -1, -2), jnp.swapaxes(state_gla[:, j, 1], -1, -2))
            o_f, o_b, _, _ = _gla_scan(y_s, z, wg, bg, init, t_p, bs, ls, low_col_block, prev_out=(of_p, ob_p))
            new_gla.append(jnp.stack([jnp.swapaxes(sf, -1, -2), jnp.swapaxes(sb, -1, -2)], axis=1))
            heads, dvh, og_blk = H_B, DV_B, 2
            gnw, w_out = gla_gnorm_w[j], gla_w_out[j]
        x = _outproj(o_f, o_b, z, og_blk, gnw, w_out.astype(BF16), x, mod3, stream, l, heads, dvh)

        q, h = _norm_mod_matmul(x, norm2_w[l], mod3, stream, l, 3, 4, peer_w_q[l].astype(BF16), 1024, emit_h=True)
        idx, gates = _route(q, peer_keys[l])
        n_exp = peer_u.shape[1]
        uv3 = jnp.concatenate([peer_u[l].reshape(n_exp, D_TILES, LANE), peer_v[l].reshape(n_exp, D_TILES, LANE)], axis=1)
        x = _peer(idx.T, gates.T, h, x, mod3, stream, l, uv3)

    y_prompt = _rms_norm(x, final_norm_w, 0, t_p).reshape(bp, lp, d)
    y_sample = _rms_norm(x, final_norm_w, t_p, t_s).reshape(bs, ls, d)
    return (y_prompt, y_sample, jnp.stack(new_hgrn, axis=1), jnp.stack(new_gla, axis=1))
```

```python
import functools

import jax
import jax.numpy as jnp
from jax import lax
from jax.experimental import pallas as pl
from jax.experimental.pallas import tpu as pltpu

F32 = jnp.float32
BF16 = jnp.bfloat16

D_MODEL = 1024
DEPTH = 4
GRID_W = 64
EPS = 1e-6
F_MIN = 1e-20
H_A, DK_A, DV_A = 8, 128, 128
H_B, DK_B, DV_B = 4, 128, 256
QKV_B = 2048
GLA_RANK = 16
GATE_NORM = 16.0
P_HEADS, N_KEYS, P_TOPK, P_HALF = 8, 128, 16, 128
N_SEL = P_HEADS * P_TOPK

V7X_VMEM_LIMIT = 48 * 1024 * 1024
LANE = 128
ROW_TILE = 1024
CHUNK = 64
SUB = 16
MOD_ROWS = 16
NEG_INF = float("-inf")


def _cparams(sem):
    return pltpu.CompilerParams(dimension_semantics=sem, vmem_limit_bytes=V7X_VMEM_LIMIT)


def _silu(x):
    return x * jax.nn.sigmoid(x)


def _gelu(x):
    return 0.5 * x * (1.0 + lax.erf(x * (2.0 ** -0.5)))


def _dot(a, b):
    return jnp.dot(a, b, preferred_element_type=F32)


def _dot_nt(a, b):
    return lax.dot_general(a, b, (((1,), (1,)), ((), ())), preferred_element_type=F32)


def _dot_tn(a, b):
    return lax.dot_general(a, b, (((0,), (0,)), ((), ())), preferred_element_type=F32)


def _dot_01(m01, x):
    x1 = x.astype(BF16)
    r1 = x - x1.astype(F32)
    x2 = r1.astype(BF16)
    x3 = (r1 - x2.astype(F32)).astype(BF16)
    return _dot(m01, x1) + _dot(m01, x2) + _dot(m01, x3)


def _adaln_kernel(c_ref, w_ref, b_ref, o_ref):
    o_ref[0] = _dot(_silu(c_ref[...]), w_ref[0]) + b_ref[0]


def _adaln(cvec, ada_w, ada_b):
    tn = 1536
    n = 6 * D_MODEL
    return pl.pallas_call(
        _adaln_kernel,
        grid=(DEPTH, n // tn),
        in_specs=[
            pl.BlockSpec((MOD_ROWS, D_MODEL), lambda l, j: (0, 0)),
            pl.BlockSpec((1, D_MODEL, tn), lambda l, j: (l, 0, j)),
            pl.BlockSpec((1, 1, tn), lambda l, j: (l, 0, j)),
        ],
        out_specs=pl.BlockSpec((1, MOD_ROWS, tn), lambda l, j: (l, 0, j)),
        out_shape=jax.ShapeDtypeStruct((DEPTH, MOD_ROWS, n), F32),
        compiler_params=_cparams(("parallel", "parallel")),
        name="adaln",
    )(cvec, ada_w, ada_b.reshape(DEPTH, 1, n))


class _Stream:
    def __init__(self, t_prompt, seq_sample, tm):
        assert t_prompt % tm == 0 and seq_sample % tm == 0
        self.tm = tm
        self.n_prompt_blocks = t_prompt // tm
        self.blocks_per_sample = seq_sample // tm

    def mod_index(self, layer, k):
        npb, bps = self.n_prompt_blocks, self.blocks_per_sample

        def f(i, *_):
            row = jnp.where(i < npb, 0, 1 + (i - npb) // bps)
            return ((layer * MOD_ROWS + row) * 6 + k, 0, 0)

        return f


def _norm_mod(x, nw, shift, scale):
    ms = jnp.mean(x * x, axis=-1, keepdims=True)
    return (x * lax.rsqrt(ms + EPS) * nw) * (1.0 + scale) + shift


def _nmm_kernel(emit_h, x_ref, nw_ref, sh_ref, sc_ref, w_ref, *rest):
    if emit_h:
        o_ref, h_ref, hs_ref = rest
    else:
        o_ref, hs_ref = rest

    @pl.when(pl.program_id(1) == 0)
    def _():
        h = _norm_mod(x_ref[...], nw_ref[...], sh_ref[0], sc_ref[0])
        hs_ref[...] = h.astype(BF16)
        if emit_h:
            h_ref[...] = h

    o_ref[...] = _dot(hs_ref[...], w_ref[...])


def _norm_mod_matmul(x, nw, mod3, stream, layer, k_shift, k_scale, w_bf16, tn, emit_h=False):
    t, d = x.shape
    n = w_bf16.shape[1]
    tm = stream.tm
    assert n % tn == 0
    out_shape = [jax.ShapeDtypeStruct((t, n), F32)]
    out_specs = [pl.BlockSpec((tm, tn), lambda i, j: (i, j))]
    if emit_h:
        out_shape.append(jax.ShapeDtypeStruct((t, d), F32))
        out_specs.append(pl.BlockSpec((tm, d), lambda i, j: (i, 0)))
    res = pl.pallas_call(
        functools.partial(_nmm_kernel, emit_h),
        grid=(t // tm, n // tn),
        in_specs=[
            pl.BlockSpec((tm, d), lambda i, j: (i, 0)),
            pl.BlockSpec((1, d), lambda i, j: (0, 0)),
            pl.BlockSpec((1, 1, d), stream.mod_index(layer, k_shift)),
            pl.BlockSpec((1, 1, d), stream.mod_index(layer, k_scale)),
            pl.BlockSpec((d, tn), lambda i, j: (0, j)),
        ],
        out_specs=out_specs,
        out_shape=out_shape,
        scratch_shapes=[pltpu.VMEM((tm, d), BF16)],
        compiler_params=_cparams(("parallel", "arbitrary")),
        name="norm_mod_matmul",
    )(x, nw.reshape(1, d), mod3, mod3, w_bf16)
    return res if emit_h else res[0]


def _scan_direction(q, k, v, g, st_ref, sidx, rev):
    c = CHUNK
    row = lax.broadcasted_iota(jnp.int32, (c, c), 0)
    col = lax.broadcasted_iota(jnp.int32, (c, c), 1)
    cmat = jnp.where((col >= row) if rev else (col <= row), 1.0, 0.0).astype(BF16)
    cs = _dot_01(cmat, g)
    st = st_ref[sidx]
    o = _dot_nt(q * jnp.exp(cs), st)
    edge = cs[0:1] if rev else cs[c - 1:c]
    st_ref[sidx] = st * jnp.exp(edge) + _dot_tn(v, k * jnp.exp(edge - cs))

    srow = lax.broadcasted_iota(jnp.int32, (SUB, c), 0)
    scol = lax.broadcasted_iota(jnp.int32, (SUB, c), 1)
    att_rows = []
    for blk in range(c // SUB):
        r0 = blk * SUB
        q_i, k_i, cs_i = q[r0:r0 + SUB], k[r0:r0 + SUB], cs[r0:r0 + SUB]
        has_off = (blk < c // SUB - 1) if rev else (blk > 0)
        if has_off:
            ref = cs[r0 + SUB:r0 + SUB + 1] if rev else cs[r0 - 1:r0]
            w_i = cs_i - ref
            kt = k * jnp.exp(jnp.minimum(ref - cs, 0.0))
            off = _dot_nt(q_i * jnp.exp(w_i), kt)
            off_mask = (scol >= r0 + SUB) if rev else (scol < r0)
            a_i = jnp.where(off_mask, off, 0.0)
        else:
            w_i = cs_i
            a_i = jnp.zeros((SUB, c), F32)
        for s in range(SUB):
            e = jnp.exp(jnp.minimum(w_i - w_i[s:s + 1], 0.0))
            colv = jnp.sum(q_i * e * k_i[s:s + 1], axis=1, keepdims=True)
            keep = (srow <= s) if rev else (srow >= s)
            a_i = jnp.where((scol == r0 + s) & keep, colv, a_i)
        att_rows.append(a_i)
    att = jnp.concatenate(att_rows, axis=0)
    return o + _dot(att, v)


def _log_sigmoid(x):
    return jnp.minimum(x, 0.0) - jnp.log(1.0 + jnp.exp(-jnp.abs(x)))


SCAN_HPB = 8


def _scan_kernel(mode, zero_init, n_aliased, hpb, dk, dv, *refs):
    if mode == "hgrn":
        (qf, vf, ff, qb, vb, fb, lbf, lbb), rest = refs[:8], refs[8:]
    else:
        (qf, kf, vf, lowf, qb, kb, vb, lowb, wgf, wgb, bgf, bgb), rest = refs[:12], refs[12:]
    if not zero_init:
        (s0f, s0b), rest = rest[:2], rest[2:]
    rest = rest[n_aliased:]
    of_ref, ob_ref, sf_ref, sb_ref, st_ref = rest
    ci = pl.program_id(2)

    @pl.when(ci == 0)
    def _():
        if zero_init:
            st_ref[...] = jnp.zeros_like(st_ref)
        else:
            for j in range(hpb):
                st_ref[j] = s0f[0, j]
                st_ref[hpb + j] = s0b[0, j]

    def prep(ks, q_ref, k_ref, f_ref, lb_ref, low_ref, wg_ref, bg_ref):
        if mode == "hgrn":
            lb = lb_ref[:, ks]
            fp = f_ref[:, ks]
            f = lb + (1.0 - lb) * jax.nn.sigmoid(fp)
            g = jnp.log(jnp.maximum(f, F_MIN))
            k = (1.0 - lb) * jax.nn.sigmoid(-fp)
            q = _silu(q_ref[:, ks])
        else:
            gk = _dot(low_ref[...], wg_ref[:, ks]) + bg_ref[:, ks]
            g = _log_sigmoid(gk) * (1.0 / GATE_NORM)
            k = k_ref[:, ks]
            q = q_ref[:, ks] * (DK_B ** -0.5)
        return q, k, g

    for j in range(hpb):
        ks, vs = slice(j * dk, (j + 1) * dk), slice(j * dv, (j + 1) * dv)
        if mode == "hgrn":
            q, k, g = prep(ks, qf, None, ff, lbf, None, None, None)
            of_ref[:, vs] = _scan_direction(q, k, vf[:, vs], g, st_ref, j, False)
            q, k, g = prep(ks, qb, None, fb, lbb, None, None, None)
            ob_ref[:, vs] = _scan_direction(q, k, vb[:, vs], g, st_ref, hpb + j, True)
        else:
            q, k, g = prep(ks, qf, kf, None, None, lowf, wgf, bgf)
            of_ref[:, vs] = _scan_direction(q, k, vf[:, vs], g, st_ref, j, False)
            q, k, g = prep(ks, qb, kb, None, None, lowb, wgb, bgb)
            ob_ref[:, vs] = _scan_direction(q, k, vb[:, vs], g, st_ref, hpb + j, True)

    @pl.when(ci == pl.num_programs(2) - 1)
    def _():
        for j in range(hpb):
            sf_ref[0, j] = st_ref[j]
            sb_ref[0, j] = st_ref[hpb + j]


def _scan(mode, arrays, consts, init_states, batch, seq, heads, dk, dv, out_rows, out_row0, prev_out):
    n = seq // CHUNK
    hpb = min(SCAN_HPB, heads)
    assert heads % hpb == 0 and out_row0 % CHUNK == 0
    zero_init = init_states is None
    orb0 = out_row0 // CHUNK

    def chunk_of(c, dirn):
        return c if dirn == 0 else n - 1 - c

    in_specs, operands = [], []
    for dirn in (0, 1):
        for arr, width, row0, cols, per_head in arrays:
            assert row0 % CHUNK == 0 and (not per_head or cols[dirn] % hpb == 0)

            def imap(b, hg, c, dirn=dirn, rb0=row0 // CHUNK, col0=cols[dirn], per_head=per_head):
                return (rb0 + b * n + chunk_of(c, dirn), (col0 // hpb + hg) if per_head else col0)

            in_specs.append(pl.BlockSpec((CHUNK, width * hpb if per_head else width), imap))
            operands.append(arr)
    for arr, shape in consts:
        in_specs.append(pl.BlockSpec((shape[0], shape[1] * hpb), lambda b, hg, c: (0, hg)))
        operands.append(arr)
    s_spec = pl.BlockSpec((1, hpb, dv, dk), lambda b, hg, c: (b, hg, 0, 0))
    if not zero_init:
        for s0 in init_states:
            in_specs.append(s_spec)
            operands.append(s0)
    aliases = {}
    if prev_out is not None:
        for k, arr in enumerate(prev_out):
            aliases[len(operands)] = k
            in_specs.append(pl.BlockSpec(memory_space=pl.ANY))
            operands.append(arr)
    o_specs = [pl.BlockSpec((CHUNK, dv * hpb), lambda b, hg, c, dirn=dirn: (orb0 + b * n + chunk_of(c, dirn), hg))
               for dirn in (0, 1)]
    return pl.pallas_call(
        functools.partial(_scan_kernel, mode, zero_init, len(aliases), hpb, dk, dv),
        grid=(batch, heads // hpb, n),
        in_specs=in_specs,
        out_specs=o_specs + [s_spec, s_spec],
        out_shape=[
            jax.ShapeDtypeStruct((out_rows, heads * dv), F32),
            jax.ShapeDtypeStruct((out_rows, heads * dv), F32),
            jax.ShapeDtypeStruct((batch, heads, dv, dk), F32),
            jax.ShapeDtypeStruct((batch, heads, dv, dk), F32),
        ],
        scratch_shapes=[pltpu.VMEM((2 * hpb, dv, dk), F32)],
        input_output_aliases=aliases,
        compiler_params=_cparams(("parallel", "parallel", "arbitrary")),
        name="scan_" + mode,
    )(*operands)


def _hgrn_scan(z, lb, init_states, row0, batch, seq, prev_out=None):
    nh = H_A
    arrays = [
        (z, DK_A, row0, (0, 0), True),
        (z, DV_A, row0, (nh, nh), True),
        (z, DK_A, row0, (2 * nh, 3 * nh), True),
    ]
    consts = [(lb[0:1], (1, DK_A)), (lb[1:2], (1, DK_A))]
    return _scan("hgrn", arrays, consts, init_states, batch, seq, H_A, DK_A, DV_A, z.shape[0], row0, prev_out)


def _gla_scan(y, z, wg, bg, init_states, row0, batch, seq, low_col_block, prev_out=None):
    v0 = 2 * H_B * DK_B // DV_B
    arrays = [
        (y, DK_B, 0, (0, 0), True),
        (y, DK_B, 0, (H_B, H_B), True),
        (y, DV_B, 0, (v0, v0), True),
        (z, LANE, row0, (low_col_block, low_col_block), False),
    ]
    consts = [(wg[0], (LANE, DK_B)), (wg[1], (LANE, DK_B)), (bg[0:1], (1, DK_B)), (bg[1:2], (1, DK_B))]
    return _scan("gla", arrays, consts, init_states, batch, seq, H_B, DK_B, DV_B, z.shape[0], row0, prev_out)


def _conv_kernel(rows, width, x_ref, w_ref, o_ref):
    seq = rows * width
    x = x_ref[...]
    assert width & (width - 1) == 0
    l = lax.broadcasted_iota(jnp.int32, (seq, 1), 0)
    r, c = l >> (width.bit_length() - 1), l & (width - 1)
    acc = jnp.zeros_like(x)
    for di in range(3):
        if rows == 1 and di != 1:
            continue
        for dj in range(3):
            k = (di - 1) * width + (dj - 1)
            xs = x if k == 0 else pltpu.roll(x, (-k) % seq, 0)
            ok = (r + (di - 1) >= 0) & (r + (di - 1) < rows) & (c + (dj - 1) >= 0) & (c + (dj - 1) < width)
            acc = acc + jnp.where(ok, xs, 0.0) * w_ref[3 * di + dj:3 * di + dj + 1, :]
    o_ref[...] = _silu(acc)


def _conv_silu(z, conv_w9, row0, batch, rows, width, tc=256):
    seq = rows * width
    sb0 = row0 // seq
    assert row0 % seq == 0
    return pl.pallas_call(
        functools.partial(_conv_kernel, rows, width),
        grid=(batch, QKV_B // tc),
        in_specs=[
            pl.BlockSpec((seq, tc), lambda b, j: (sb0 + b, j)),
            pl.BlockSpec((9, tc), lambda b, j: (0, j)),
        ],
        out_specs=pl.BlockSpec((seq, tc), lambda b, j: (b, j)),
        out_shape=jax.ShapeDtypeStruct((batch * seq, QKV_B), F32),
        compiler_params=_cparams(("parallel", "parallel")),
        name="conv_silu",
    )(z, conv_w9)


def _outproj_kernel(heads, dvh, of_ref, ob_ref, og_ref, gw_ref, w_ref, x_ref, g_ref, o_ref):
    o = of_ref[...] + ob_ref[...]
    og = og_ref[...]
    gw = gw_ref[...]
    parts = []
    for h in range(heads):
        oh = o[:, h * dvh:(h + 1) * dvh]
        ms = jnp.mean(oh * oh, axis=-1, keepdims=True)
        parts.append(oh * lax.rsqrt(ms + EPS) * gw)
    y = jnp.concatenate(parts, axis=1) * _silu(og)
    o_ref[...] = x_ref[...] + g_ref[0] * _dot(y.astype(BF16), w_ref[...])


def _outproj(o_f, o_b, z, og_col_block, gnorm_w, w_out_bf16, x, mod3, stream, layer, heads, dvh):
    t, d = x.shape
    tm = 256
    st = _Stream(stream.n_prompt_blocks * stream.tm, stream.blocks_per_sample * stream.tm, tm)
    return pl.pallas_call(
        functools.partial(_outproj_kernel, heads, dvh),
        grid=(t // tm,),
        in_specs=[
            pl.BlockSpec((tm, d), lambda i: (i, 0)),
            pl.BlockSpec((tm, d), lambda i: (i, 0)),
            pl.BlockSpec((tm, d), lambda i: (i, og_col_block)),
            pl.BlockSpec((1, dvh), lambda i: (0, 0)),
            pl.BlockSpec((d, d), lambda i: (0, 0)),
            pl.BlockSpec((tm, d), lambda i: (i, 0)),
            pl.BlockSpec((1, 1, d), st.mod_index(layer, 2)),
        ],
        out_specs=pl.BlockSpec((tm, d), lambda i: (i, 0)),
        out_shape=jax.ShapeDtypeStruct((t, d), F32),
        compiler_params=_cparams(("parallel",)),
        name="outproj",
    )(o_f, o_b, z, gnorm_w.reshape(1, dvh), w_out_bf16, x, mod3)


def _top_rows(scores, k):
    n, tm = scores[0].shape
    rid = lax.broadcasted_iota(jnp.int32, (n, tm), 0).astype(F32)
    kid = lax.broadcasted_iota(jnp.int32, (k, tm), 0)

    def body(j, carry):
        out = []
        for s, vals, ids in carry:
            m = jnp.max(s, axis=0, keepdims=True)
            pos = jnp.min(jnp.where(s == m, rid, float(n)), axis=0, keepdims=True)
            vals = jnp.where(kid == j, m, vals)
            ids = jnp.where(kid == j, pos, ids)
            out.append((jnp.where(rid == pos, NEG_INF, s), vals, ids))
        return tuple(out)

    init = tuple((s, jnp.zeros((k, tm), F32), jnp.zeros((k, tm), F32)) for s in scores)
    res = lax.fori_loop(0, k, body, init)
    return [r[1] for r in res], [r[2] for r in res]


_PAIR_COUNTS = [P_TOPK // (a + 1) for a in range(P_TOPK)]
_N_PAIRS = sum(_PAIR_COUNTS)
_N_PAIR_ROWS = -(-_N_PAIRS // 8) * 8


ROUTE_TOKENS = 512
ROUTE_SUB = 256


def _route_kernel(hb_ref, wq_ref, keys_ref, idx_ref, gate_ref, cand_ref, cid_ref):
    for c0 in range(0, hb_ref.shape[0], ROUTE_SUB):
        q = _dot(hb_ref[c0:c0 + ROUTE_SUB, :], wq_ref[...])
        _route_tile(q, keys_ref, idx_ref.at[:, c0:c0 + ROUTE_SUB], gate_ref.at[:, c0:c0 + ROUTE_SUB],
                    cand_ref, cid_ref)


def _route_tile(q, keys_ref, idx_ref, gate_ref, cand_ref, cid_ref):
    tm = q.shape[0]
    scores = [_dot_nt(keys_ref[0, p], q[:, p * P_HALF:(p + 1) * P_HALF]) for p in range(2)]
    vals, ids = _top_rows(scores, P_TOPK)
    cand_ref[...] = jnp.full(cand_ref.shape, NEG_INF, F32)
    cid_ref[...] = jnp.zeros(cid_ref.shape, F32)
    r0 = 0
    for a, nb in enumerate(_PAIR_COUNTS):
        cand_ref[r0:r0 + nb, :] = vals[0][a:a + 1] + vals[1][0:nb]
        cid_ref[r0:r0 + nb, :] = ids[0][a:a + 1] * float(N_KEYS) + ids[1][0:nb]
        r0 += nb
    cand = cand_ref[...]
    cid = cid_ref[...]
    n = cand.shape[0]
    rid = lax.broadcasted_iota(jnp.int32, (n, tm), 0).astype(F32)
    kid = lax.broadcasted_iota(jnp.int32, (P_TOPK, tm), 0)

    def body(j, carry):
        cand, top, idx = carry
        m = jnp.max(cand, axis=0, keepdims=True)
        pos = jnp.min(jnp.where(cand == m, rid, float(n)), axis=0, keepdims=True)
        sel = rid == pos
        e = jnp.sum(jnp.where(sel, cid, 0.0), axis=0, keepdims=True)
        top = jnp.where(kid == j, m, top)
        idx = jnp.where(kid == j, e, idx)
        return jnp.where(sel, NEG_INF, cand), top, idx

    _, top, idx = lax.fori_loop(
        0, P_TOPK, body, (cand, jnp.zeros((P_TOPK, tm), F32), jnp.zeros((P_TOPK, tm), F32)))
    ex = jnp.exp(top - top[0:1])
    gate_ref[...] = ex / jnp.sum(ex, axis=0, keepdims=True)
    idx_ref[...] = idx.astype(jnp.int32)


def _norm_mod_kernel(x_ref, nw_ref, sh_ref, sc_ref, h_ref, hb_ref):
    h = _norm_mod(x_ref[...], nw_ref[...], sh_ref[0], sc_ref[0])
    h_ref[...] = h
    hb_ref[...] = h.astype(BF16)


def _norm_mod_pair(x, nw, mod3, stream, layer, k_shift, k_scale):
    t, d = x.shape
    tm = stream.tm
    spec = pl.BlockSpec((tm, d), lambda i: (i, 0))
    return pl.pallas_call(
        _norm_mod_kernel,
        grid=(t // tm,),
        in_specs=[
            spec,
            pl.BlockSpec((1, d), lambda i: (0, 0)),
            pl.BlockSpec((1, 1, d), stream.mod_index(layer, k_shift)),
            pl.BlockSpec((1, 1, d), stream.mod_index(layer, k_scale)),
        ],
        out_specs=[spec, spec],
        out_shape=[jax.ShapeDtypeStruct((t, d), F32), jax.ShapeDtypeStruct((t, d), BF16)],
        compiler_params=_cparams(("parallel",)),
        name="norm_mod",
    )(x, nw.reshape(1, d), mod3, mod3)


def _route(hb, wq_bf16, keys):
    t, d = hb.shape
    tm = ROUTE_TOKENS
    return pl.pallas_call(
        _route_kernel,
        grid=(P_HEADS, t // tm),
        in_specs=[
            pl.BlockSpec((tm, d), lambda h, i: (i, 0)),
            pl.BlockSpec((d, 2 * P_HALF), lambda h, i: (0, h)),
            pl.BlockSpec((1, 2, N_KEYS, P_HALF), lambda h, i: (h, 0, 0, 0)),
        ],
        out_specs=[
            pl.BlockSpec((P_TOPK, tm), lambda h, i: (h, i)),
            pl.BlockSpec((P_TOPK, tm), lambda h, i: (h, i)),
        ],
        out_shape=[
            jax.ShapeDtypeStruct((N_SEL, t), jnp.int32),
            jax.ShapeDtypeStruct((N_SEL, t), F32),
        ],
        scratch_shapes=[pltpu.VMEM((_N_PAIR_ROWS, ROUTE_SUB), F32), pltpu.VMEM((_N_PAIR_ROWS, ROUTE_SUB), F32)],
        compiler_params=_cparams(("parallel", "parallel")),
        name="peer_route",
    )(hb, wq_bf16, keys)


PEER_TB = 128
PEER_NBUF = 8
SUBLANES = 8
D_TILES = D_MODEL // LANE


def _sublane_sums(p):
    s = lax.broadcasted_iota(jnp.int32, p.shape[2:], 0)
    sel = s < 4
    a, b = p[:, 0:4], p[:, 4:8]
    f = jnp.where(sel, a, b) + pltpu.roll(jnp.where(sel, b, a), 4, 2)
    sel = (s & 2) == 0
    a, b = f[:, 0:2], f[:, 2:4]
    g = jnp.where(sel, a + pltpu.roll(a, 6, 2), b + pltpu.roll(b, 2, 2))
    sel = (s & 1) == 0
    a, b = g[:, 0], g[:, 1]
    return jnp.where(sel, a + pltpu.roll(a, 7, 1), b + pltpu.roll(b, 1, 1))


def _peer_kernel(idx_ref, idx_next_ref, h_ref, gate_ref, x_ref, g2_ref, uv_hbm, o_ref, *scratch):
    bufs, cb_ref, sem = scratch[:PEER_NBUF], scratch[PEER_NBUF], scratch[PEER_NBUF + 1]
    step, nsteps = pl.program_id(0), pl.num_programs(0)

    def start_token(ids_ref, t, slot):
        for e in range(N_SEL):
            pltpu.make_async_copy(uv_hbm.at[ids_ref[t, e]], bufs[slot].at[e], sem.at[slot]).start(priority=e % 2)

    def wait_token(slot):
        pltpu.make_async_copy(uv_hbm.at[pl.ds(0, N_SEL)], bufs[slot], sem.at[slot]).wait()

    eye = lax.broadcasted_iota(jnp.int32, (N_SEL, N_SEL), 0) == lax.broadcasted_iota(jnp.int32, (N_SEL, N_SEL), 1)
    g2 = g2_ref[0]

    def compute(t, slot):
        buf = bufs[slot]
        prod = buf[:, 0:SUBLANES, :] * h_ref[t]
        rows = _sublane_sums(prod.reshape(N_SEL // 8, 8, SUBLANES, LANE))
        s = jnp.sum(rows.reshape(N_SEL, LANE), axis=1, keepdims=True)
        gcol = jnp.sum(jnp.where(eye, gate_ref[pl.ds(t, 1), :], 0.0), axis=1, keepdims=True)
        cb_ref[...] = jnp.broadcast_to(gcol * _gelu(s), (N_SEL, LANE))
        accs = [jnp.zeros((SUBLANES, LANE), F32) for _ in range(4)]
        for e in range(N_SEL):
            accs[e % 4] = accs[e % 4] + cb_ref[pl.ds(e, 1), :] * buf[e, SUBLANES:2 * SUBLANES, :]
        o_ref[t] = x_ref[t] + g2 * ((accs[0] + accs[1]) + (accs[2] + accs[3]))

    ahead = PEER_NBUF - 1

    @pl.when(step == 0)
    def _():
        for t in range(ahead):
            start_token(idx_ref, t, t)

    def group(t0, last):
        for k in range(PEER_NBUF):
            wait_token(k)
            nxt, slot = t0 + k + ahead, (k + ahead) % PEER_NBUF
            if not last:
                start_token(idx_ref, nxt, slot)
            elif k + ahead < PEER_NBUF:
                start_token(idx_ref, PEER_TB - PEER_NBUF + k + ahead, slot)
            else:
                @pl.when(step + 1 < nsteps)
                def _():
                    start_token(idx_next_ref, k + ahead - PEER_NBUF, slot)
            compute(t0 + k, k)

    def body(i, _):
        group(PEER_NBUF * i, False)
        return 0

    lax.fori_loop(0, PEER_TB // PEER_NBUF - 1, body, 0)
    group(PEER_TB - PEER_NBUF, True)


def _peer(idx_t, gate_t, h, x, mod3, stream, layer, uv3):
    t, d = x.shape
    tb = PEER_TB
    st = _Stream(stream.n_prompt_blocks * stream.tm, stream.blocks_per_sample * stream.tm, tb)
    tok_spec = pl.BlockSpec((tb, D_TILES, LANE), lambda i: (i, 0, 0))
    last_block = t // tb - 1
    out = pl.pallas_call(
        _peer_kernel,
        grid=(t // tb,),
        in_specs=[
            pl.BlockSpec((tb, N_SEL), lambda i: (i, 0), memory_space=pltpu.SMEM),
            pl.BlockSpec((tb, N_SEL), lambda i: (jnp.minimum(i + 1, last_block), 0), memory_space=pltpu.SMEM),
            tok_spec,
            pl.BlockSpec((tb, N_SEL), lambda i: (i, 0)),
            tok_spec,
            pl.BlockSpec((1, D_TILES, LANE), st.mod_index(layer, 5)),
            pl.BlockSpec(memory_space=pl.ANY),
        ],
        out_specs=tok_spec,
        out_shape=jax.ShapeDtypeStruct((t, D_TILES, LANE), F32),
        scratch_shapes=[pltpu.VMEM((N_SEL, 2 * SUBLANES, LANE), F32) for _ in range(PEER_NBUF)] + [
            pltpu.VMEM((N_SEL, LANE), F32),
            pltpu.SemaphoreType.DMA((PEER_NBUF,)),
        ],
        compiler_params=_cparams(("arbitrary",)),
        name="peer_experts",
    )(idx_t, idx_t, h.reshape(t, D_TILES, LANE), gate_t, x.reshape(t, D_TILES, LANE),
      mod3.reshape(-1, D_TILES, LANE), uv3)
    return out.reshape(t, d)


def _rms_kernel(x_ref, w_ref, o_ref):
    x = x_ref[...]
    ms = jnp.mean(x * x, axis=-1, keepdims=True)
    o_ref[...] = x * lax.rsqrt(ms + EPS) * w_ref[...]


def _rms_norm(x, w, row0, rows):
    d = x.shape[1]
    tm = 512
    assert row0 % tm == 0 and rows % tm == 0
    return pl.pallas_call(
        _rms_kernel,
        grid=(rows // tm,),
        in_specs=[pl.BlockSpec((tm, d), lambda i: (row0 // tm + i, 0)), pl.BlockSpec((1, d), lambda i: (0, 0))],
        out_specs=pl.BlockSpec((tm, d), lambda i: (i, 0)),
        out_shape=jax.ShapeDtypeStruct((rows, d), F32),
        compiler_params=_cparams(("parallel",)),
        name="final_norm",
    )(x, w.reshape(1, d))


def kernel(x_prompt, x_sample, state_hgrn, state_gla, c, c_ctx, ada_w, ada_b, norm1_w, norm2_w, final_norm_w, hgrn_w_in, hgrn_lb, hgrn_gnorm_w, hgrn_w_out, gla_w_in, gla_conv_w, gla_w_gate, gla_b_gate, gla_gnorm_w, gla_w_out, peer_w_q, peer_keys, peer_u, peer_v):
    bp, lp, d = x_prompt.shape
    bs, ls, _ = x_sample.shape
    t_p, t_s = bp * lp, bs * ls
    rows_s = ls // GRID_W
    stream = _Stream(t_p, ls, ROW_TILE)

    x = jnp.concatenate([x_prompt.reshape(t_p, d), x_sample.reshape(t_s, d)], axis=0)

    cvec = jnp.concatenate([c_ctx[None, :], c, jnp.zeros((MOD_ROWS - 1 - bs, d), F32)], axis=0)
    mod3 = _adaln(cvec, ada_w, ada_b).reshape(DEPTH * MOD_ROWS * 6, 1, d)

    p = jax.nn.softmax(hgrn_lb.astype(F32), axis=0)
    lower_bounds = jnp.cumsum(p, axis=0) - p[0:1]

    gla_n = gla_w_in.shape[2]
    gla_np = -(-gla_n // LANE) * LANE
    low_col_block = (QKV_B + D_MODEL) // LANE
    n_b = gla_w_in.shape[0]
    wg_pad = jnp.zeros((n_b, 2, LANE, H_B * DK_B), F32)
    wg_pad = wg_pad.at[:, 0, 0:GLA_RANK].set(gla_w_gate[:, 0]).at[:, 1, GLA_RANK:2 * GLA_RANK].set(gla_w_gate[:, 1])

    o_zero = jnp.zeros((t_p + t_s, d), F32)

    new_hgrn, new_gla = [], []
    for l in range(DEPTH):
        j = l // 2
        if l % 2 == 0:
            z = _norm_mod_matmul(x, norm1_w[l], mod3, stream, l, 0, 1, hgrn_w_in[j].astype(BF16), 1024)
            lb = lower_bounds[j]
            of_p, ob_p, sf, sb = _hgrn_scan(z, lb, None, 0, bp, lp, prev_out=(o_zero, o_zero))
            init = (jnp.swapaxes(state_hgrn[:, j, 0], -1, -2), jnp.swapaxes(state_hgrn[:, j, 1], -1, -2))
            o_f, o_b, _, _ = _hgrn_scan(z, lb, init, t_p, bs, ls, prev_out=(of_p, ob_p))
            new_hgrn.append(jnp.stack([jnp.swapaxes(sf, -1, -2), jnp.swapaxes(sb, -1, -2)], axis=1))
            heads, dvh, og_blk = H_A, DV_A, 4
            gnw, w_out = hgrn_gnorm_w[j], hgrn_w_out[j]
        else:
            w_in = jnp.pad(gla_w_in[j], ((0, 0), (0, gla_np - gla_n))).astype(BF16)
            z = _norm_mod_matmul(x, norm1_w[l], mod3, stream, l, 0, 1, w_in, gla_np // 5 if gla_np % 5 == 0 else LANE)
            cw = gla_conv_w[j].reshape(9, QKV_B)
            y_p = _conv_silu(z, cw, 0, bp, 1, lp)
            y_s = _conv_silu(z, cw, t_p, bs, rows_s, GRID_W)
            wg, bg = wg_pad[j], gla_b_gate[j]
            of_p, ob_p, sf, sb = _gla_scan(y_p, z, wg, bg, None, 0, bp, lp, low_col_block, prev_out=(o_zero, o_zero))
            init = (jnp.swapaxes(state_gla[:, j, 0], -1, -2), jnp.swapaxes(state_gla[:, j, 1], -1, -2))
            o_f, o_b, _, _ = _gla_scan(y_s, z, wg, bg, init, t_p, bs, ls, low_col_block, prev_out=(of_p, ob_p))
            new_gla.append(jnp.stack([jnp.swapaxes(sf, -1, -2), jnp.swapaxes(sb, -1, -2)], axis=1))
            heads, dvh, og_blk = H_B, DV_B, 2
            gnw, w_out = gla_gnorm_w[j], gla_w_out[j]
        x = _outproj(o_f, o_b, z, og_blk, gnw, w_out.astype(BF16), x, mod3, stream, l, heads, dvh)

        h, hb = _norm_mod_pair(x, norm2_w[l], mod3, stream, l, 3, 4)
        idx, gates = _route(hb, peer_w_q[l].astype(BF16), peer_keys[l])
        n_exp = peer_u.shape[1]
        uv3 = jnp.concatenate([peer_u[l].reshape(n_exp, D_TILES, LANE), peer_v[l].reshape(n_exp, D_TILES, LANE)], axis=1)
        x = _peer(idx.T, gates.T, h, x, mod3, stream, l, uv3)

    y_prompt = _rms_norm(x, final_norm_w, 0, t_p).reshape(bp, lp, d)
    y_sample = _rms_norm(x, final_norm_w, t_p, t_s).reshape(bs, ls, d)
    return (y_prompt, y_sample, jnp.stack(new_hgrn, axis=1), jnp.stack(new_gla, axis=1))
```
